```python
import jax, jax.numpy as jnp
from jax import lax
import numpy as np

D_MODEL = 2048
BATCH = 4
SEQ = 2048
DEPTH = 1
DEC_BATCH = 8
DEC_SEQ = 1
PAST_LEN = 16384
PAGE_SIZE = 128

N_HEADS = 16
HEAD_DIM = D_MODEL // N_HEADS
D_ATTN = N_HEADS * HEAD_DIM
D_CONV = D_MODEL // 2
CONV_W = 3
D_FF = -(-8 * D_MODEL // (3 * 256)) * 256
Q_BLOCK = 128
N_MOD = 6
EPS = 1e-6
SB_BIAS_INIT = -6.0
IN_SIZES = (D_ATTN, D_ATTN, D_ATTN, D_CONV, D_CONV, D_CONV, D_MODEL, D_MODEL)
D_IN = sum(IN_SIZES)
IN_OFFSETS = tuple(int(o) for o in np.cumsum(IN_SIZES)[:-1])

kernel_name = "stickbreak_shortconv_adaln_hybrid_step"


def rmsnorm(x, g):
    xf = x.astype(jnp.float32)
    r = lax.rsqrt(jnp.mean(xf * xf, axis=-1, keepdims=True) + EPS)
    return (xf * r).astype(x.dtype) * g


def stick_breaking(q, k, v, bias, q_idx, k_idx):
    z = (jnp.einsum('bqhd,bkhd->bhqk', q, k).astype(jnp.float32) * (HEAD_DIM ** -0.5)
         + bias.astype(jnp.float32)[None, :, None, None])
    causal = k_idx[None, :] < q_idx[:, None]
    log_keep = jnp.where(causal, jax.nn.log_sigmoid(-z), 0.0)
    suffix = lax.cumsum(log_keep, axis=3, reverse=True) - log_keep
    w = jnp.where(causal, jnp.exp(jax.nn.log_sigmoid(z) + suffix), 0.0)
    return jnp.einsum('bhqk,bkhd->bqhd', w.astype(v.dtype), v)


def prompt_attend(q, k, v, bias):
    b, s, h, d = q.shape
    nb = s // Q_BLOCK
    qb = q.reshape(b, nb, Q_BLOCK, h, d).transpose(1, 0, 2, 3, 4)
    k_idx = jnp.arange(s)

    def one_block(args):
        qi, i = args
        q_idx = i * Q_BLOCK + jnp.arange(Q_BLOCK)
        return stick_breaking(qi, k, v, bias, q_idx, k_idx)

    out = lax.map(one_block, (qb, jnp.arange(nb)))
    return out.transpose(1, 0, 2, 3, 4).reshape(b, s, h, d)


def make_sample_attend(past_k, past_v):
    def attend(q, k, v, bias):
        p = past_k.shape[1]
        t = q.shape[1]
        kk = jnp.concatenate([past_k.astype(k.dtype), k], axis=1)
        vv = jnp.concatenate([past_v.astype(v.dtype), v], axis=1)
        return stick_breaking(q, kk, vv, bias, p + jnp.arange(t), jnp.arange(p + t))
    return attend


def mixer(h, conv_prefix, attend, w_in, b_sb, conv_w, w_attn_out, w_conv_out, w_out):
    b, l, _ = h.shape
    q, k, v, cb, cc, ch, ga, gb = jnp.split(h @ w_in, IN_OFFSETS, axis=-1)
    q = q.reshape(b, l, N_HEADS, HEAD_DIM)
    k = k.reshape(b, l, N_HEADS, HEAD_DIM)
    v = v.reshape(b, l, N_HEADS, HEAD_DIM)
    o_attn = attend(q, k, v, b_sb).reshape(b, l, D_ATTN)
    u = cc * ch
    u_full = jnp.concatenate([conv_prefix.astype(u.dtype), u], axis=1)
    conv = sum(conv_w[j] * u_full[:, j:j + l] for j in range(CONV_W))
    o_conv = cb * conv
    merged = jax.nn.sigmoid(ga) * (o_attn @ w_attn_out) + jax.nn.sigmoid(gb) * (o_conv @ w_conv_out)
    return merged @ w_out, k, v, u_full[:, -(CONV_W - 1):]


def decoder_layer(x, c, conv_prefix, attend, g_mix, g_ffn, w_mod, b_mod, w_in, b_sb, conv_w,
                  w_attn_out, w_conv_out, w_out, w_ffn_in, w_ffn_out):
    mod = (jax.nn.silu(c) @ w_mod + b_mod)[:, None, :]
    shift1, scale1, gate1, shift2, scale2, gate2 = jnp.split(mod, N_MOD, axis=-1)
    h = rmsnorm(x, g_mix) * (1 + scale1) + shift1
    mix_out, k, v, conv_state = mixer(h, conv_prefix, attend, w_in, b_sb, conv_w,
                                      w_attn_out, w_conv_out, w_out)
    x = x + gate1 * mix_out
    h = rmsnorm(x, g_ffn) * (1 + scale2) + shift2
    a, bb = jnp.split(h @ w_ffn_in, 2, axis=-1)
    x = x + gate2 * ((jax.nn.silu(a) * bb) @ w_ffn_out)
    return x, k, v, conv_state


def setup_inputs(seed: int = 0) -> dict:
    key = jax.random.key(seed)
    ks = jax.random.split(key, 24)
    n_pages = PAST_LEN // PAGE_SIZE
    n_used = DEC_BATCH * n_pages
    n_pool = n_used + max(1, n_used // 4)
    f32 = jnp.float32

    def nrm(k, shape, scale):
        return jax.random.normal(k, shape, f32) * scale

    perm = jax.random.permutation(ks[0], n_pool)[:n_used]
    page_table = perm.reshape(DEC_BATCH, n_pages).astype(jnp.int32)
    return {
        "x_prompt": nrm(ks[1], (BATCH, SEQ, D_MODEL), 1.0),
        "x_sample": nrm(ks[2], (DEC_BATCH, DEC_SEQ, D_MODEL), 1.0),
        "cache_k": nrm(ks[3], (DEPTH, n_pool, PAGE_SIZE, N_HEADS, HEAD_DIM), 1.0),
        "cache_v": nrm(ks[4], (DEPTH, n_pool, PAGE_SIZE, N_HEADS, HEAD_DIM), 1.0),
        "state_conv": nrm(ks[5], (DEPTH, DEC_BATCH, CONV_W - 1, D_CONV), 1.0),
        "page_table": page_table,
        "c_prompt": nrm(ks[6], (BATCH, D_MODEL), 1.0),
        "c_sample": nrm(ks[7], (DEC_BATCH, D_MODEL), 1.0),
        "g_mix": 1.0 + nrm(ks[8], (DEPTH, D_MODEL), 0.01),
        "g_ffn": 1.0 + nrm(ks[9], (DEPTH, D_MODEL), 0.01),
        "g_final": 1.0 + nrm(ks[10], (D_MODEL,), 0.01),
        "w_mod": nrm(ks[11], (DEPTH, D_MODEL, N_MOD * D_MODEL), 0.5 * D_MODEL ** -0.5),
        "b_mod": nrm(ks[12], (DEPTH, N_MOD * D_MODEL), 0.01),
        "w_in": nrm(ks[13], (DEPTH, D_MODEL, D_IN), D_MODEL ** -0.5),
        "b_sb": SB_BIAS_INIT + nrm(ks[20], (DEPTH, N_HEADS), 0.1),
        "conv_w": nrm(ks[14], (DEPTH, CONV_W, D_CONV), CONV_W ** -0.5),
        "w_attn_out": nrm(ks[15], (DEPTH, D_ATTN, D_MODEL), D_ATTN ** -0.5),
        "w_conv_out": nrm(ks[16], (DEPTH, D_CONV, D_MODEL), D_CONV ** -0.5),
        "w_out": nrm(ks[17], (DEPTH, D_MODEL, D_MODEL), D_MODEL ** -0.5),
        "w_ffn_in": nrm(ks[18], (DEPTH, D_MODEL, 2 * D_FF), D_MODEL ** -0.5),
        "w_ffn_out": nrm(ks[19], (DEPTH, D_FF, D_MODEL), D_FF ** -0.5),
    }


def reference(x_prompt, x_sample, cache_k, cache_v, state_conv, page_table, c_prompt, c_sample,
              g_mix, g_ffn, g_final, w_mod, b_mod, w_in, b_sb, conv_w, w_attn_out, w_conv_out,
              w_out, w_ffn_in, w_ffn_out):
    xp, xs = x_prompt, x_sample
    db = page_table.shape[0]
    k_p, v_p, cs_p, k_s, v_s, cs_s = [], [], [], [], [], []
    for l in range(DEPTH):
        lw = (g_mix[l], g_ffn[l], w_mod[l], b_mod[l], w_in[l], b_sb[l], conv_w[l],
              w_attn_out[l], w_conv_out[l], w_out[l], w_ffn_in[l], w_ffn_out[l])
        prefix0 = jnp.zeros((xp.shape[0], CONV_W - 1, D_CONV), xp.dtype)
        xp, kp, vp, csp = decoder_layer(xp, c_prompt, prefix0, prompt_attend, *lw)
        past_k = cache_k[l][page_table].reshape(db, -1, N_HEADS, HEAD_DIM)
        past_v = cache_v[l][page_table].reshape(db, -1, N_HEADS, HEAD_DIM)
        xs, ks_, vs_, css = decoder_layer(xs, c_sample, state_conv[l],
                                          make_sample_attend(past_k, past_v), *lw)
        k_p.append(kp); v_p.append(vp); cs_p.append(csp)
        k_s.append(ks_); v_s.append(vs_); cs_s.append(css)
    y_prompt = rmsnorm(xp, g_final)
    y_sample = rmsnorm(xs, g_final)
    return (y_prompt, y_sample, jnp.stack(k_p), jnp.stack(v_p), jnp.stack(cs_p),
            jnp.stack(k_s), jnp.stack(v_s), jnp.stack(cs_s))
```

```python
import functools

import jax
import jax.numpy as jnp
from jax import lax
from jax.experimental import pallas as pl
from jax.experimental.pallas import tpu as pltpu

F32 = jnp.float32
BF16 = jnp.bfloat16

EPS = 1e-6
N_MOD = 6
CONV_W = 3
HEAD_DIM = 128
LANES = 128
SUBLANES = 8
SAMPLE_ROWS = 16
VMEM_LIMIT = 56 * 1024 * 1024
CAST_ROWS = 256


def _params(n_axes):
    return pltpu.CompilerParams(
        dimension_semantics=("arbitrary",) * n_axes,
        vmem_limit_bytes=VMEM_LIMIT)


def _softplus(z):
    return jnp.maximum(z, 0.0) + jnp.log1p(jnp.exp(-jnp.abs(z)))


def _sigmoid(z):
    return 1.0 / (1.0 + jnp.exp(-z))


def _split_bf16(x):
    hi = x.astype(BF16)
    lo = (x - hi.astype(F32)).astype(BF16)
    return hi, lo


def _rms(x):
    return x * lax.rsqrt(jnp.mean(x * x, axis=-1, keepdims=True) + EPS)


def _prenorm_kernel(x_ref, g_ref, scale_ref, shift_ref, h_ref):
    h = _rms(x_ref[...]) * g_ref[...]
    h_ref[...] = (h * (1.0 + scale_ref[...]) + shift_ref[...]).astype(h_ref.dtype)


def _finalnorm_kernel(x_ref, g_ref, y_ref):
    y_ref[...] = _rms(x_ref[...]) * g_ref[...]


def _row_tile(s):
    return min(s, 512)


def _prenorm(x, g, scale, shift):
    b, s, d = x.shape
    tm = _row_tile(s)
    r = scale.shape[1]
    mod_spec = pl.BlockSpec((None, r, d), lambda i, j: (i, 0, 0))
    return pl.pallas_call(
        _prenorm_kernel,
        grid=(b, s // tm),
        in_specs=[pl.BlockSpec((None, tm, d), lambda i, j: (i, j, 0)),
                  pl.BlockSpec((1, d), lambda i, j: (0, 0)),
                  mod_spec, mod_spec],
        out_specs=pl.BlockSpec((None, tm, d), lambda i, j: (i, j, 0)),
        out_shape=jax.ShapeDtypeStruct((b, s, d), BF16),
        compiler_params=_params(2),
        name="prenorm",
    )(x, g, scale, shift)


def _finalnorm(x, g):
    b, s, d = x.shape
    tm = _row_tile(s)
    return pl.pallas_call(
        _finalnorm_kernel,
        grid=(b, s // tm),
        in_specs=[pl.BlockSpec((None, tm, d), lambda i, j: (i, j, 0)),
                  pl.BlockSpec((1, d), lambda i, j: (0, 0))],
        out_specs=pl.BlockSpec((None, tm, d), lambda i, j: (i, j, 0)),
        out_shape=jax.ShapeDtypeStruct((b, s, d), F32),
        compiler_params=_params(2),
        name="finalnorm",
    )(x, g)


def _cast_weight(wb_ref, w_ref):
    def body(i, carry):
        r = pl.multiple_of(i * CAST_ROWS, CAST_ROWS)
        wb_ref[pl.ds(r, CAST_ROWS), :] = w_ref[pl.ds(r, CAST_ROWS), :].astype(BF16)
        return carry
    lax.fori_loop(0, w_ref.shape[0] // CAST_ROWS, body, 0)


def _ws_kernel(*refs, body, n_act, n_w, n_in, n_out):
    acts = refs[:n_act]
    ws = refs[n_act:n_act + n_w]
    others = refs[n_act + n_w:n_in]
    outs = refs[n_in:n_in + n_out]
    scratch = refs[n_in + n_out:]
    wbs = scratch[:n_w]

    @pl.when(pl.program_id(1) == 0)
    def _():
        for wb, w in zip(wbs, ws):
            _cast_weight(wb, w)

    body(acts, wbs, others, outs, scratch[n_w:])


def _ws_call(body, *, name, m, tm, tn, n_tiles, tiles_per_batch, acts, weights,
             rows=(), tiles=(), outs=(), extra_outs=(), extra_scratch=()):
    tpb = tiles_per_batch
    in_specs, args = [], []
    for a in acts:
        in_specs.append(pl.BlockSpec((tm, a.shape[1]), lambda n, i: (i, 0)))
        args.append(a)
    for w, off in weights:
        in_specs.append(pl.BlockSpec((w.shape[0], tn), lambda n, i, off=off: (0, off + n)))
        args.append(w)
    for r in rows:
        in_specs.append(pl.BlockSpec((None, r.shape[1], tn), lambda n, i: (i // tpb, 0, n)))
        args.append(r)
    for t in tiles:
        in_specs.append(pl.BlockSpec((tm, tn), lambda n, i: (i, n)))
        args.append(t)
    out_specs = [pl.BlockSpec((tm, tn), lambda n, i: (i, n)) for _ in outs]
    out_shape = [jax.ShapeDtypeStruct((m, cols), dt) for cols, dt in outs]
    for shape, dt, spec in extra_outs:
        out_specs.append(spec)
        out_shape.append(jax.ShapeDtypeStruct(shape, dt))
    scratch = [pltpu.VMEM((w.shape[0], tn), BF16) for w, _ in weights] + list(extra_scratch)
    kern = functools.partial(_ws_kernel, body=body, n_act=len(acts), n_w=len(weights),
                             n_in=len(args), n_out=len(out_shape))
    return pl.pallas_call(
        kern,
        grid=(n_tiles, m // tm),
        in_specs=in_specs,
        out_specs=out_specs,
        out_shape=out_shape,
        scratch_shapes=scratch,
        compiler_params=_params(2),
        name=name,
    )(*args)


def _dot(a, wb_ref):
    return jnp.dot(a, wb_ref[...], preferred_element_type=F32)


def _mod_body(acts, wbs, others, outs, scratch):
    c = acts[0][...]
    a = (c * _sigmoid(c)).astype(BF16)
    outs[0][...] = _dot(a, wbs[0]) + others[0][...]


def _qkv_body(acts, wbs, others, outs, scratch):
    a = acts[0][...]
    outs[0][...] = _dot(a, wbs[0]).astype(BF16)
    outs[1][...] = _dot(a, wbs[1])
    outs[2][...] = _dot(a, wbs[2])


def _conv_prompt_body(acts, wbs, others, outs, scratch, *, tpb):
    cw_ref, = others
    o_ref, state_ref = outs
    ubuf, = scratch
    a = acts[0][...]
    tm = a.shape[0]
    pos = pl.program_id(1) % tpb

    @pl.when(pos == 0)
    def _():
        ubuf[0:SUBLANES, :] = jnp.zeros((SUBLANES, ubuf.shape[1]), F32)

    u = _dot(a, wbs[1]) * _dot(a, wbs[2])
    ubuf[SUBLANES:, :] = u
    cw = cw_ref[...]
    conv = (cw[0:1, :] * ubuf[SUBLANES - 2:SUBLANES - 2 + tm, :]
            + cw[1:2, :] * ubuf[SUBLANES - 1:SUBLANES - 1 + tm, :]
            + cw[2:3, :] * u)
    o_ref[...] = (_dot(a, wbs[0]) * conv).astype(o_ref.dtype)
    ubuf[0:SUBLANES, :] = u[tm - SUBLANES:, :]

    @pl.when(pos == tpb - 1)
    def _():
        state_ref[...] = u[tm - (CONV_W - 1):, :]


def _conv_sample_body(acts, wbs, others, outs, scratch):
    cw_ref, = others[:1]
    p0_ref, p1_ref = others[1:]
    o_ref, u_ref = outs
    a = acts[0][...]
    u = _dot(a, wbs[1]) * _dot(a, wbs[2])
    cw = cw_ref[...]
    conv = cw[0:1, :] * p0_ref[...] + cw[1:2, :] * p1_ref[...] + cw[2:3, :] * u
    o_ref[...] = (_dot(a, wbs[0]) * conv).astype(o_ref.dtype)
    u_ref[...] = u


def _merge_body(acts, wbs, others, outs, scratch):
    h = acts[0][...]
    o_attn = acts[1][...]
    o_conv = acts[2][...]
    ga = _sigmoid(_dot(h, wbs[0]))
    gb = _sigmoid(_dot(h, wbs[1]))
    merged = ga * _dot(o_attn, wbs[2]) + gb * _dot(o_conv, wbs[3])
    outs[0][...] = merged.astype(outs[0].dtype)


def _residual_body(acts, wbs, others, outs, scratch):
    gate_ref, x_ref = others
    outs[0][...] = x_ref[...] + gate_ref[...] * _dot(acts[0][...], wbs[0])


def _swiglu_body(acts, wbs, others, outs, scratch):
    a = acts[0][...]
    up = _dot(a, wbs[0])
    outs[0][...] = (up * _sigmoid(up) * _dot(a, wbs[1])).astype(outs[0].dtype)


def _attn_prompt_kernel(bias_ref, q_ref, k_ref, v_ref, o_ref, kb_ref, vb_ref, tri_ref, *, tq):
    head = pl.program_id(1)
    qi = pl.program_id(2)
    row = lax.broadcasted_iota(jnp.int32, (tq, tq), 0)
    col = lax.broadcasted_iota(jnp.int32, (tq, tq), 1)

    @pl.when(qi == 0)
    def _():
        kb_ref[...] = k_ref[...].astype(BF16)
        vb_ref[...] = v_ref[...].astype(BF16)
        tri_ref[...] = jnp.where(row > col, 1.0, 0.0).astype(BF16)

    bias = bias_ref[0, head]
    q = q_ref[...]
    causal = col < row
    scale = HEAD_DIM ** -0.5

    def block(start, run, diagonal):
        kblk = kb_ref[pl.ds(start, tq), :]
        vblk = vb_ref[pl.ds(start, tq), :]
        z = lax.dot_general(q, kblk, (((1,), (1,)), ((), ())),
                            preferred_element_type=F32) * scale + bias
        sp = _softplus(z)
        log_keep = -sp
        if diagonal:
            log_keep = jnp.where(causal, log_keep, 0.0)
        hi, lo = _split_bf16(log_keep)
        tri = tri_ref[...]
        suffix = (jnp.dot(hi, tri, preferred_element_type=F32)
                  + jnp.dot(lo, tri, preferred_element_type=F32))
        w = jnp.exp((z - sp) + suffix + run)
        if diagonal:
            w = jnp.where(causal, w, 0.0)
        pv = jnp.dot(w.astype(BF16), vblk, preferred_element_type=F32)
        return pv, run + jnp.sum(log_keep, axis=1, keepdims=True)

    acc, run = block(pl.multiple_of(qi * tq, tq), jnp.zeros((tq, 1), F32), True)

    def body(j, carry):
        acc, run = carry
        pv, run = block(pl.multiple_of((qi - 1 - j) * tq, tq), run, False)
        return acc + pv, run

    acc, run = lax.fori_loop(0, qi, body, (acc, run))
    o_ref[...] = acc.astype(o_ref.dtype)


def _attn_prompt(q, k, v, bias, *, tq=256):
    b, s, d = q.shape
    n_heads = d // HEAD_DIM
    kv_spec = pl.BlockSpec((None, s, HEAD_DIM), lambda i, h, j: (i, 0, h))
    q_spec = pl.BlockSpec((None, tq, HEAD_DIM), lambda i, h, j: (i, j, h))
    return pl.pallas_call(
        functools.partial(_attn_prompt_kernel, tq=tq),
        grid=(b, n_heads, s // tq),
        in_specs=[pl.BlockSpec(memory_space=pltpu.SMEM), q_spec, kv_spec, kv_spec],
        out_specs=q_spec,
        out_shape=jax.ShapeDtypeStruct((b, s, d), BF16),
        scratch_shapes=[pltpu.VMEM((s, HEAD_DIM), BF16),
                        pltpu.VMEM((s, HEAD_DIM), BF16),
                        pltpu.VMEM((tq, tq), BF16)],
        compiler_params=_params(3),
        name="attn_prompt",
    )(bias, q, k, v)


DECODE_SLOTS = 3


def _attn_decode_kernel(pt_ref, bias_ref, q_ref, ck_ref, cv_ref, o_ref,
                        kbuf, vbuf, sem, tri_ref, acc_ref, run_ref):
    seq = pl.program_id(0)
    n_seq = pl.num_programs(0)
    n_pages = pt_ref.shape[1]
    n_heads, page, _ = kbuf.shape[1:]
    total = n_seq * n_pages

    def page_copies(j):
        page_id = pt_ref[j // n_pages, n_pages - 1 - j % n_pages]
        slot = j % DECODE_SLOTS
        copies = []
        for h in range(n_heads):
            copies.append(pltpu.make_async_copy(
                ck_ref.at[page_id, :, h, :], kbuf.at[slot, h], sem.at[0, slot]))
            copies.append(pltpu.make_async_copy(
                cv_ref.at[page_id, :, h, :], vbuf.at[slot, h], sem.at[1, slot]))
        return copies

    def start(j):
        for c in page_copies(j):
            c.start()

    def wait(j):
        for c in page_copies(j):
            c.wait()

    head_row = lax.broadcasted_iota(jnp.int32, (n_heads, page), 0)

    @pl.when(seq == 0)
    def _():
        for j in range(DECODE_SLOTS - 1):
            start(j)
        pr = lax.broadcasted_iota(jnp.int32, (page, page), 0)
        pc = lax.broadcasted_iota(jnp.int32, (page, page), 1)
        tri_ref[...] = jnp.where(pr > pc, 1.0, 0.0).astype(BF16)

    acc_ref[...] = jnp.zeros(acc_ref.shape, F32)
    run_ref[...] = jnp.zeros(run_ref.shape, F32)
    qb = q_ref[...].astype(BF16)
    bias = bias_ref[...]
    scale = HEAD_DIM ** -0.5

    def body(p, carry):
        j = seq * n_pages + p
        wait(j)
        nxt = j + DECODE_SLOTS - 1

        @pl.when(nxt < total)
        def _():
            start(nxt)

        slot = j % DECODE_SLOTS
        kall = kbuf[slot].reshape(n_heads * page, HEAD_DIM).astype(BF16)
        vall = vbuf[slot].reshape(n_heads * page, HEAD_DIM).astype(BF16)
        cross = lax.dot_general(qb, kall, (((1,), (1,)), ((), ())),
                                preferred_element_type=F32)
        s = jnp.zeros((n_heads, page), F32)
        for t in range(n_heads):
            s = s + jnp.where(head_row == t, cross[:, t * page:(t + 1) * page], 0.0)
        z = s * scale + bias
        sp = _softplus(z)
        log_keep = -sp
        hi, lo = _split_bf16(log_keep)
        tri = tri_ref[...]
        suffix = (jnp.dot(hi, tri, preferred_element_type=F32)
                  + jnp.dot(lo, tri, preferred_element_type=F32))
        run = run_ref[...]
        w = jnp.exp((z - sp) + suffix + run)
        run_ref[...] = run + jnp.sum(log_keep, axis=1, keepdims=True)
        w_blocks = jnp.concatenate(
            [jnp.where(head_row == t, w, 0.0) for t in range(n_heads)], axis=1).astype(BF16)
        acc_ref[...] += jnp.dot(w_blocks, vall, preferred_element_type=F32)
        return carry

    lax.fori_loop(0, n_pages, body, 0)
    o_ref[...] = acc_ref[...]


def _attn_decode(q, cache_k, cache_v, page_table, bias_rows):
    db, n_heads, _ = q.shape
    page = cache_k.shape[1]
    seq_spec = pl.BlockSpec((None, n_heads, HEAD_DIM), lambda b, pt: (b, 0, 0))
    slot_shape = (DECODE_SLOTS, n_heads, page, HEAD_DIM)
    grid_spec = pltpu.PrefetchScalarGridSpec(
        num_scalar_prefetch=1,
        grid=(db,),
        in_specs=[pl.BlockSpec((n_heads, page), lambda b, pt: (0, 0)), seq_spec,
                  pl.BlockSpec(memory_space=pl.ANY), pl.BlockSpec(memory_space=pl.ANY)],
        out_specs=seq_spec,
        scratch_shapes=[pltpu.VMEM(slot_shape, F32),
                        pltpu.VMEM(slot_shape, F32),
                        pltpu.SemaphoreType.DMA((2, DECODE_SLOTS)),
                        pltpu.VMEM((page, page), BF16),
                        pltpu.VMEM((n_heads, HEAD_DIM), F32),
                        pltpu.VMEM((n_heads, 1), F32)])
    return pl.pallas_call(
        _attn_decode_kernel,
        grid_spec=grid_spec,
        out_shape=jax.ShapeDtypeStruct((db, n_heads, HEAD_DIM), F32),
        compiler_params=_params(1),
        name="attn_decode",
    )(page_table, bias_rows, q, cache_k, cache_v)


def _col_tiles(offset, tn):
    assert offset % tn == 0
    return offset // tn


def _layer(x, mods, lw, attend, conv, *, tm):
    (g_mix, g_ffn, w_in, w_attn_out, w_conv_out, w_out, w_ffn_in, w_ffn_out) = lw
    shift1, scale1, gate1, shift2, scale2, gate2 = mods
    b, s, d = x.shape
    m = b * s
    tpb = s // tm
    d_attn = w_attn_out.shape[0]
    d_conv = w_conv_out.shape[0]
    d_ff = w_ffn_out.shape[0]
    off_k, off_v = d_attn, 2 * d_attn
    off_cb = 3 * d_attn
    off_cc, off_ch = off_cb + d_conv, off_cb + 2 * d_conv
    off_ga = off_cb + 3 * d_conv
    off_gb = off_ga + d
    tn = 512
    common = dict(m=m, tm=tm, tiles_per_batch=tpb)

    h = _prenorm(x, g_mix, scale1, shift1).reshape(m, d)
    q, k, v = _ws_call(
        _qkv_body, name="qkv", tn=tn, n_tiles=d_attn // tn, acts=[h],
        weights=[(w_in, 0), (w_in, _col_tiles(off_k, tn)), (w_in, _col_tiles(off_v, tn))],
        outs=[(d_attn, BF16), (d_attn, F32), (d_attn, F32)], **common)
    o_attn = attend(q, k, v)
    tnc = 256
    conv_w = [(w_in, _col_tiles(off_cb, tnc)), (w_in, _col_tiles(off_cc, tnc)),
              (w_in, _col_tiles(off_ch, tnc))]
    o_conv, conv_state = conv(h, conv_w, tnc, common)
    merged, = _ws_call(
        _merge_body, name="merge", tn=tn, n_tiles=d // tn, acts=[h, o_attn, o_conv],
        weights=[(w_in, _col_tiles(off_ga, tn)), (w_in, _col_tiles(off_gb, tn)),
                 (w_attn_out, 0), (w_conv_out, 0)],
        outs=[(d, BF16)], **common)
    x1, = _ws_call(
        _residual_body, name="mix_out", tn=tn, n_tiles=d // tn, acts=[merged],
        weights=[(w_out, 0)], rows=[gate1], tiles=[x.reshape(m, d)],
        outs=[(d, F32)], **common)
    h2 = _prenorm(x1.reshape(b, s, d), g_ffn, scale2, shift2).reshape(m, d)
    act, = _ws_call(
        _swiglu_body, name="ffn_in", tn=tn, n_tiles=d_ff // tn, acts=[h2],
        weights=[(w_ffn_in, 0), (w_ffn_in, _col_tiles(d_ff, tn))],
        outs=[(d_ff, BF16)], **common)
    x2, = _ws_call(
        _residual_body, name="ffn_out", tn=tn, n_tiles=d // tn, acts=[act],
        weights=[(w_ffn_out, 0)], rows=[gate2], tiles=[x1],
        outs=[(d, F32)], **common)
    return x2.reshape(b, s, d), k, v, conv_state


def kernel(x_prompt, x_sample, cache_k, cache_v, state_conv, page_table, c_prompt, c_sample,
           g_mix, g_ffn, g_final, w_mod, b_mod, w_in, b_sb, conv_w, w_attn_out, w_conv_out,
           w_out, w_ffn_in, w_ffn_out):
    depth = w_in.shape[0]
    bp, sp, d = x_prompt.shape
    db, ds, _ = x_sample.shape
    assert ds == 1 and db <= SAMPLE_ROWS
    n_heads = b_sb.shape[1]
    d_attn = n_heads * HEAD_DIM
    d_conv = conv_w.shape[2]
    n_pool, page = cache_k.shape[1], cache_k.shape[2]
    pad = SAMPLE_ROWS - db

    def pad_rows(a):
        return jnp.pad(a, ((0, pad), (0, 0)))

    xp = x_prompt
    xs = pad_rows(x_sample.reshape(db, d)).reshape(1, SAMPLE_ROWS, d)
    c_all = jnp.concatenate([c_prompt, c_sample], axis=0)
    c_all = jnp.pad(c_all, ((0, -c_all.shape[0] % SAMPLE_ROWS), (0, 0)))
    g_final2 = g_final.reshape(1, d)
    outs = {k: [] for k in ("kp", "vp", "cp", "ks", "vs", "cs")}

    for l in range(depth):
        mod, = _ws_call(
            _mod_body, name="mod", m=c_all.shape[0], tm=c_all.shape[0], tn=1024,
            n_tiles=N_MOD * d // 1024, tiles_per_batch=1, acts=[c_all],
            weights=[(w_mod[l], 0)], rows=[b_mod[l].reshape(1, 1, N_MOD * d)],
            outs=[(N_MOD * d, F32)])
        mods_p = [mod[:bp, i * d:(i + 1) * d].reshape(bp, 1, d) for i in range(N_MOD)]
        mods_s = [pad_rows(mod[bp:bp + db, i * d:(i + 1) * d]).reshape(1, SAMPLE_ROWS, d)
                  for i in range(N_MOD)]
        lw = (g_mix[l].reshape(1, d), g_ffn[l].reshape(1, d), w_in[l], w_attn_out[l],
              w_conv_out[l], w_out[l], w_ffn_in[l], w_ffn_out[l])
        cw = jnp.pad(conv_w[l], ((0, SUBLANES - CONV_W), (0, 0)))
        bias_smem = b_sb[l].reshape(1, n_heads)
        bias_rows = jnp.broadcast_to(b_sb[l].reshape(n_heads, 1), (n_heads, page))

        def attend_p(q, k, v):
            shp = (bp, sp, d_attn)
            return _attn_prompt(q.reshape(shp), k.reshape(shp), v.reshape(shp),
                                bias_smem).reshape(bp * sp, d_attn)

        def conv_p(h, conv_weights, tnc, common):
            tm = common["tm"]
            tpb = common["tiles_per_batch"]
            state_spec = pl.BlockSpec((None, CONV_W - 1, tnc), lambda n, i: (i // tpb, 0, n))
            cw_spec_rows = cw.reshape(1, SUBLANES, d_conv)
            return _ws_call(
                functools.partial(_conv_prompt_body, tpb=tpb), name="conv_prompt", tn=tnc,
                n_tiles=d_conv // tnc, acts=[h], weights=conv_weights,
                rows=[jnp.broadcast_to(cw_spec_rows, (bp, SUBLANES, d_conv))],
                outs=[(d_conv, BF16)],
                extra_outs=[((bp, CONV_W - 1, d_conv), F32, state_spec)],
                extra_scratch=[pltpu.VMEM((tm + SUBLANES, tnc), F32)], **common)

        xp, kp, vp, csp = _layer(xp, mods_p, lw, attend_p, conv_p, tm=512)

        ck, cv = cache_k[l], cache_v[l]

        def attend_s(q, k, v):
            q3 = q[:db].astype(F32).reshape(db, n_heads, HEAD_DIM)
            o = _attn_decode(q3, ck, cv, page_table, bias_rows)
            return pad_rows(o.reshape(db, d_attn)).astype(BF16)

        p0 = pad_rows(state_conv[l][:, 0, :])
        p1 = pad_rows(state_conv[l][:, 1, :])

        def conv_s(h, conv_weights, tnc, common):
            o_conv, u = _ws_call(
                _conv_sample_body, name="conv_sample", tn=tnc, n_tiles=d_conv // tnc,
                acts=[h], weights=conv_weights, rows=[cw.reshape(1, SUBLANES, d_conv)],
                tiles=[p0, p1], outs=[(d_conv, BF16), (d_conv, F32)], **common)
            return o_conv, jnp.stack([p1[:db], u[:db]], axis=1)

        xs, ks, vs, css = _layer(xs, mods_s, lw, attend_s, conv_s, tm=SAMPLE_ROWS)

        outs["kp"].append(kp.reshape(bp, sp, n_heads, HEAD_DIM))
        outs["vp"].append(vp.reshape(bp, sp, n_heads, HEAD_DIM))
        outs["cp"].append(csp)
        outs["ks"].append(ks[:db].reshape(db, 1, n_heads, HEAD_DIM))
        outs["vs"].append(vs[:db].reshape(db, 1, n_heads, HEAD_DIM))
        outs["cs"].append(css)

    y_prompt = _finalnorm(xp, g_final2)
    y_sample = _finalnorm(xs, g_final2)[0, :db].reshape(db, 1, d)
    return (y_prompt, y_sample, jnp.stack(outs["kp"]), jnp.stack(outs["vp"]),
            jnp.stack(outs["cp"]), jnp.stack(outs["ks"]), jnp.stack(outs["vs"]),
            jnp.stack(outs["cs"]))
```

```python
import functools

import jax
import jax.numpy as jnp
from jax import lax
from jax.experimental import pallas as pl
from jax.experimental.pallas import tpu as pltpu

F32 = jnp.float32
BF16 = jnp.bfloat16

EPS = 1e-6
N_MOD = 6
CONV_W = 3
HEAD_DIM = 128
LANES = 128
SUBLANES = 8
SAMPLE_ROWS = 16
VMEM_LIMIT = 56 * 1024 * 1024
CAST_ROWS = 256


def _params(n_axes):
    return pltpu.CompilerParams(
        dimension_semantics=("arbitrary",) * n_axes,
        vmem_limit_bytes=VMEM_LIMIT)


LOG2E = 1.4426950408889634


def _softplus2(z2):
    return jnp.maximum(z2, 0.0) + jnp.log2(1.0 + jnp.exp2(-jnp.abs(z2)))


def _sigmoid(z):
    return 1.0 / (1.0 + jnp.exp(-z))


def _split_bf16(x):
    hi = x.astype(BF16)
    lo = (x - hi.astype(F32)).astype(BF16)
    return hi, lo


def _rms(x):
    return x * lax.rsqrt(jnp.mean(x * x, axis=-1, keepdims=True) + EPS)


def _prenorm_kernel(x_ref, g_ref, scale_ref, shift_ref, h_ref):
    h = _rms(x_ref[...]) * g_ref[...]
    h_ref[...] = (h * (1.0 + scale_ref[...]) + shift_ref[...]).astype(h_ref.dtype)


def _finalnorm_kernel(x_ref, g_ref, y_ref):
    y_ref[...] = _rms(x_ref[...]) * g_ref[...]


def _row_tile(s):
    return min(s, 512)


def _prenorm(x, g, scale, shift):
    b, s, d = x.shape
    tm = _row_tile(s)
    r = scale.shape[1]
    mod_spec = pl.BlockSpec((None, r, d), lambda i, j: (i, 0, 0))
    return pl.pallas_call(
        _prenorm_kernel,
        grid=(b, s // tm),
        in_specs=[pl.BlockSpec((None, tm, d), lambda i, j: (i, j, 0)),
                  pl.BlockSpec((1, d), lambda i, j: (0, 0)),
                  mod_spec, mod_spec],
        out_specs=pl.BlockSpec((None, tm, d), lambda i, j: (i, j, 0)),
        out_shape=jax.ShapeDtypeStruct((b, s, d), BF16),
        compiler_params=_params(2),
        name="prenorm",
    )(x, g, scale, shift)


def _finalnorm(x, g):
    b, s, d = x.shape
    tm = _row_tile(s)
    return pl.pallas_call(
        _finalnorm_kernel,
        grid=(b, s // tm),
        in_specs=[pl.BlockSpec((None, tm, d), lambda i, j: (i, j, 0)),
                  pl.BlockSpec((1, d), lambda i, j: (0, 0))],
        out_specs=pl.BlockSpec((None, tm, d), lambda i, j: (i, j, 0)),
        out_shape=jax.ShapeDtypeStruct((b, s, d), F32),
        compiler_params=_params(2),
        name="finalnorm",
    )(x, g)


def _cast_weight(wb_ref, w_ref):
    def body(i, carry):
        r = pl.multiple_of(i * CAST_ROWS, CAST_ROWS)
        wb_ref[pl.ds(r, CAST_ROWS), :] = w_ref[pl.ds(r, CAST_ROWS), :].astype(BF16)
        return carry
    lax.fori_loop(0, w_ref.shape[0] // CAST_ROWS, body, 0)


def _ws_kernel(*refs, body, n_act, n_w, n_in, n_out):
    acts = refs[:n_act]
    ws = refs[n_act:n_act + n_w]
    others = refs[n_act + n_w:n_in]
    outs = refs[n_in:n_in + n_out]
    scratch = refs[n_in + n_out:]
    wbs = scratch[:n_w]

    @pl.when(pl.program_id(1) == 0)
    def _():
        for wb, w in zip(wbs, ws):
            _cast_weight(wb, w)

    body(acts, wbs, others, outs, scratch[n_w:])


def _ws_call(body, *, name, m, tm, tn, n_tiles, tiles_per_batch, acts, weights,
             rows=(), tiles=(), outs=(), extra_outs=(), extra_scratch=()):
    tpb = tiles_per_batch
    in_specs, args = [], []
    for a in acts:
        in_specs.append(pl.BlockSpec((tm, a.shape[1]), lambda n, i: (i, 0)))
        args.append(a)
    for w, off in weights:
        in_specs.append(pl.BlockSpec((w.shape[0], tn), lambda n, i, off=off: (0, off + n)))
        args.append(w)
    for r in rows:
        in_specs.append(pl.BlockSpec((None, r.shape[1], tn), lambda n, i: (i // tpb, 0, n)))
        args.append(r)
    for t in tiles:
        in_specs.append(pl.BlockSpec((tm, tn), lambda n, i: (i, n)))
        args.append(t)
    out_specs = [pl.BlockSpec((tm, tn), lambda n, i: (i, n)) for _ in outs]
    out_shape = [jax.ShapeDtypeStruct((m, cols), dt) for cols, dt in outs]
    for shape, dt, spec in extra_outs:
        out_specs.append(spec)
        out_shape.append(jax.ShapeDtypeStruct(shape, dt))
    scratch = [pltpu.VMEM((w.shape[0], tn), BF16) for w, _ in weights] + list(extra_scratch)
    kern = functools.partial(_ws_kernel, body=body, n_act=len(acts), n_w=len(weights),
                             n_in=len(args), n_out=len(out_shape))
    return pl.pallas_call(
        kern,
        grid=(n_tiles, m // tm),
        in_specs=in_specs,
        out_specs=out_specs,
        out_shape=out_shape,
        scratch_shapes=scratch,
        compiler_params=_params(2),
        name=name,
    )(*args)


def _dot(a, wb_ref):
    return jnp.dot(a, wb_ref[...], preferred_element_type=F32)


def _mod_body(acts, wbs, others, outs, scratch):
    c = acts[0][...]
    a = (c * _sigmoid(c)).astype(BF16)
    outs[0][...] = _dot(a, wbs[0]) + others[0][...]


def _qkv_body(acts, wbs, others, outs, scratch):
    a = acts[0][...]
    outs[0][...] = _dot(a, wbs[0]).astype(BF16)
    outs[1][...] = _dot(a, wbs[1])
    outs[2][...] = _dot(a, wbs[2])


def _conv_prompt_body(acts, wbs, others, outs, scratch, *, tpb):
    cw_ref, = others
    o_ref, state_ref = outs
    ubuf, = scratch
    a = acts[0][...]
    tm = a.shape[0]
    pos = pl.program_id(1) % tpb

    @pl.when(pos == 0)
    def _():
        ubuf[0:SUBLANES, :] = jnp.zeros((SUBLANES, ubuf.shape[1]), F32)

    u = _dot(a, wbs[1]) * _dot(a, wbs[2])
    ubuf[SUBLANES:, :] = u
    cw = cw_ref[...]
    conv = (cw[0:1, :] * ubuf[SUBLANES - 2:SUBLANES - 2 + tm, :]
            + cw[1:2, :] * ubuf[SUBLANES - 1:SUBLANES - 1 + tm, :]
            + cw[2:3, :] * u)
    o_ref[...] = (_dot(a, wbs[0]) * conv).astype(o_ref.dtype)
    ubuf[0:SUBLANES, :] = u[tm - SUBLANES:, :]

    @pl.when(pos == tpb - 1)
    def _():
        state_ref[...] = u[tm - (CONV_W - 1):, :]


def _conv_sample_body(acts, wbs, others, outs, scratch):
    cw_ref, = others[:1]
    p0_ref, p1_ref = others[1:]
    o_ref, u_ref = outs
    a = acts[0][...]
    u = _dot(a, wbs[1]) * _dot(a, wbs[2])
    cw = cw_ref[...]
    conv = cw[0:1, :] * p0_ref[...] + cw[1:2, :] * p1_ref[...] + cw[2:3, :] * u
    o_ref[...] = (_dot(a, wbs[0]) * conv).astype(o_ref.dtype)
    u_ref[...] = u


def _merge_body(acts, wbs, others, outs, scratch):
    h = acts[0][...]
    o_attn = acts[1][...]
    o_conv = acts[2][...]
    ga = _sigmoid(_dot(h, wbs[0]))
    gb = _sigmoid(_dot(h, wbs[1]))
    merged = ga * _dot(o_attn, wbs[2]) + gb * _dot(o_conv, wbs[3])
    outs[0][...] = merged.astype(outs[0].dtype)


def _residual_body(acts, wbs, others, outs, scratch):
    gate_ref, x_ref = others
    outs[0][...] = x_ref[...] + gate_ref[...] * _dot(acts[0][...], wbs[0])


def _swiglu_body(acts, wbs, others, outs, scratch):
    a = acts[0][...]
    up = _dot(a, wbs[0])
    outs[0][...] = (up * _sigmoid(up) * _dot(a, wbs[1])).astype(outs[0].dtype)


def _attn_prompt_kernel(bias_ref, q_ref, k_ref, v_ref, o_ref, kb_ref, vb_ref, tri_ref, *, tq, hps):
    group = pl.program_id(1)
    qi = pl.program_id(2)
    row = lax.broadcasted_iota(jnp.int32, (tq, tq), 0)
    col = lax.broadcasted_iota(jnp.int32, (tq, tq), 1)

    @pl.when(qi == 0)
    def _():
        kb_ref[...] = k_ref[...].astype(BF16)
        vb_ref[...] = v_ref[...].astype(BF16)
        neg_tri = jnp.where(row > col, -1.0, 0.0).astype(BF16)
        tri_ref[...] = jnp.concatenate([neg_tri, neg_tri], axis=0)

    causal = col < row
    scale2 = HEAD_DIM ** -0.5 * LOG2E

    heads = range(hps)
    lanes = [slice(i * HEAD_DIM, (i + 1) * HEAD_DIM) for i in heads]
    bias2 = [bias_ref[0, group * hps + i] * LOG2E for i in heads]

    def blocks(start, runs, diagonal):
        z2 = [lax.dot_general(q_ref[:, lanes[i]], kb_ref[pl.ds(start, tq), lanes[i]],
                              (((1,), (1,)), ((), ())), preferred_element_type=F32)
              * scale2 + bias2[i] for i in heads]
        sp2 = [_softplus2(z) for z in z2]
        drop = [jnp.where(causal, s, 0.0) for s in sp2] if diagonal else sp2
        suffix = [jnp.dot(jnp.concatenate(_split_bf16(d), axis=1), tri_ref[...],
                          preferred_element_type=F32) for d in drop]
        w = [jnp.exp2((z2[i] - sp2[i]) + suffix[i] + runs[i]) for i in heads]
        if diagonal:
            w = [jnp.where(causal, x, 0.0) for x in w]
        pv = [jnp.dot(w[i].astype(BF16), vb_ref[pl.ds(start, tq), lanes[i]],
                      preferred_element_type=F32) for i in heads]
        runs = [runs[i] - jnp.sum(drop[i], axis=1, keepdims=True) for i in heads]
        return pv, runs

    first = pl.multiple_of(qi * tq, tq)
    accs, runs = blocks(first, [jnp.zeros((tq, 1), F32)] * hps, True)

    def body(j, carry):
        accs, runs = carry
        pv, runs = blocks(pl.multiple_of((qi - 1 - j) * tq, tq), runs, False)
        return tuple(a + p for a, p in zip(accs, pv)), tuple(runs)

    accs, _ = lax.fori_loop(0, qi, body, (tuple(accs), tuple(runs)))
    for i in heads:
        o_ref[:, lanes[i]] = accs[i].astype(o_ref.dtype)


def _attn_prompt(q, k, v, bias, *, tq=256, hps=4):
    b, s, d = q.shape
    n_heads = d // HEAD_DIM
    width = hps * HEAD_DIM
    kv_spec = pl.BlockSpec((None, s, width), lambda i, g, j: (i, 0, g))
    q_spec = pl.BlockSpec((None, tq, width), lambda i, g, j: (i, j, g))
    return pl.pallas_call(
        functools.partial(_attn_prompt_kernel, tq=tq, hps=hps),
        grid=(b, n_heads // hps, s // tq),
        in_specs=[pl.BlockSpec(memory_space=pltpu.SMEM), q_spec, kv_spec, kv_spec],
        out_specs=q_spec,
        out_shape=jax.ShapeDtypeStruct((b, s, d), BF16),
        scratch_shapes=[pltpu.VMEM((s, width), BF16),
                        pltpu.VMEM((s, width), BF16),
                        pltpu.VMEM((2 * tq, tq), BF16)],
        compiler_params=_params(3),
        name="attn_prompt",
    )(bias, q, k, v)


DECODE_GROUP = 4
DECODE_SLOTS = 2


def _attn_decode_kernel(pt_ref, bias_ref, q_ref, ck_ref, cv_ref, o_ref,
                        kbuf, vbuf, sem, tri_ref, acc_ref, run_ref):
    seq = pl.program_id(0)
    n_seq = pl.num_programs(0)
    n_pages = pt_ref.shape[1]
    grp, n_heads, page, _ = kbuf.shape[1:]
    n_groups = n_pages // grp
    total = n_seq * n_groups

    def group_copies(g):
        first_page = n_pages - (g % n_groups + 1) * grp
        slot = g % DECODE_SLOTS
        copies = []
        for i in range(grp):
            page_id = pt_ref[g // n_groups, first_page + i]
            for h in range(n_heads):
                copies.append(pltpu.make_async_copy(
                    ck_ref.at[page_id, :, h, :], kbuf.at[slot, i, h], sem.at[0, slot]))
                copies.append(pltpu.make_async_copy(
                    cv_ref.at[page_id, :, h, :], vbuf.at[slot, i, h], sem.at[1, slot]))
        return copies

    def start(g):
        for c in group_copies(g):
            c.start()

    def wait(g):
        for c in group_copies(g):
            c.wait()

    head_row = lax.broadcasted_iota(jnp.int32, (n_heads, page), 0)

    @pl.when(seq == 0)
    def _():
        start(0)
        pr = lax.broadcasted_iota(jnp.int32, (page, page), 0)
        pc = lax.broadcasted_iota(jnp.int32, (page, page), 1)
        tri_ref[...] = jnp.where(pr > pc, 1.0, 0.0).astype(BF16)

    acc_ref[...] = jnp.zeros(acc_ref.shape, F32)
    run_ref[...] = jnp.zeros(run_ref.shape, F32)
    qb = q_ref[...].astype(BF16)
    bias2 = jnp.concatenate([bias_ref[...]] * grp, axis=0) * LOG2E
    scale2 = HEAD_DIM ** -0.5 * LOG2E
    rows = grp * n_heads * page

    def body(p, carry):
        g = seq * n_groups + p
        wait(g)

        @pl.when(g + 1 < total)
        def _():
            start(g + 1)

        slot = g % DECODE_SLOTS
        kall = kbuf[slot].reshape(rows, HEAD_DIM).astype(BF16)
        vall = vbuf[slot].reshape(rows, HEAD_DIM).astype(BF16)
        cross = lax.dot_general(qb, kall, (((1,), (1,)), ((), ())),
                                preferred_element_type=F32)
        pieces = []
        for i in range(grp):
            s = jnp.zeros((n_heads, page), F32)
            for t in range(n_heads):
                c0 = (i * n_heads + t) * page
                s = s + jnp.where(head_row == t, cross[:, c0:c0 + page], 0.0)
            pieces.append(s)
        z2 = jnp.concatenate(pieces, axis=0) * scale2 + bias2
        sp2 = _softplus2(z2)
        log_keep = -sp2
        hi, lo = _split_bf16(log_keep)
        tri = tri_ref[...]
        suffix = (jnp.dot(hi, tri, preferred_element_type=F32)
                  + jnp.dot(lo, tri, preferred_element_type=F32))
        page_sum = jnp.sum(log_keep, axis=1, keepdims=True)
        tail = run_ref[...]
        later = [None] * grp
        for i in reversed(range(grp)):
            later[i] = tail
            tail = tail + page_sum[i * n_heads:(i + 1) * n_heads]
        run_ref[...] = tail
        w = jnp.exp2((z2 - sp2) + suffix + jnp.concatenate(later, axis=0))
        w_blocks = jnp.concatenate(
            [jnp.where(head_row == t, w[i * n_heads:(i + 1) * n_heads], 0.0)
             for i in range(grp) for t in range(n_heads)], axis=1).astype(BF16)
        acc_ref[...] += jnp.dot(w_blocks, vall, preferred_element_type=F32)
        return carry

    lax.fori_loop(0, n_groups, body, 0)
    o_ref[...] = acc_ref[...]


def _attn_decode(q, cache_k, cache_v, page_table, bias_rows):
    db, n_heads, _ = q.shape
    page = cache_k.shape[1]
    assert page_table.shape[1] % DECODE_GROUP == 0
    seq_spec = pl.BlockSpec((None, n_heads, HEAD_DIM), lambda b, pt: (b, 0, 0))
    slot_shape = (DECODE_SLOTS, DECODE_GROUP, n_heads, page, HEAD_DIM)
    grid_spec = pltpu.PrefetchScalarGridSpec(
        num_scalar_prefetch=1,
        grid=(db,),
        in_specs=[pl.BlockSpec((n_heads, page), lambda b, pt: (0, 0)), seq_spec,
                  pl.BlockSpec(memory_space=pl.ANY), pl.BlockSpec(memory_space=pl.ANY)],
        out_specs=seq_spec,
        scratch_shapes=[pltpu.VMEM(slot_shape, F32),
                        pltpu.VMEM(slot_shape, F32),
                        pltpu.SemaphoreType.DMA((2, DECODE_SLOTS)),
                        pltpu.VMEM((page, page), BF16),
                        pltpu.VMEM((n_heads, HEAD_DIM), F32),
                        pltpu.VMEM((n_heads, 1), F32)])
    return pl.pallas_call(
        _attn_decode_kernel,
        grid_spec=grid_spec,
        out_shape=jax.ShapeDtypeStruct((db, n_heads, HEAD_DIM), F32),
        compiler_params=_params(1),
        name="attn_decode",
    )(page_table, bias_rows, q, cache_k, cache_v)


def _col_tiles(offset, tn):
    assert offset % tn == 0
    return offset // tn


def _layer(x, mods, lw, attend, conv, *, tm):
    (g_mix, g_ffn, w_in, w_attn_out, w_conv_out, w_out, w_ffn_in, w_ffn_out) = lw
    shift1, scale1, gate1, shift2, scale2, gate2 = mods
    b, s, d = x.shape
    m = b * s
    tpb = s // tm
    d_attn = w_attn_out.shape[0]
    d_conv = w_conv_out.shape[0]
    d_ff = w_ffn_out.shape[0]
    off_k, off_v = d_attn, 2 * d_attn
    off_cb = 3 * d_attn
    off_cc, off_ch = off_cb + d_conv, off_cb + 2 * d_conv
    off_ga = off_cb + 3 * d_conv
    off_gb = off_ga + d
    tn = 512
    common = dict(m=m, tm=tm, tiles_per_batch=tpb)

    h = _prenorm(x, g_mix, scale1, shift1).reshape(m, d)
    q, k, v = _ws_call(
        _qkv_body, name="qkv", tn=tn, n_tiles=d_attn // tn, acts=[h],
        weights=[(w_in, 0), (w_in, _col_tiles(off_k, tn)), (w_in, _col_tiles(off_v, tn))],
        outs=[(d_attn, BF16), (d_attn, F32), (d_attn, F32)], **common)
    o_attn = attend(q, k, v)
    tnc = 256
    conv_w = [(w_in, _col_tiles(off_cb, tnc)), (w_in, _col_tiles(off_cc, tnc)),
              (w_in, _col_tiles(off_ch, tnc))]
    o_conv, conv_state = conv(h, conv_w, tnc, common)
    merged, = _ws_call(
        _merge_body, name="merge", tn=tn, n_tiles=d // tn, acts=[h, o_attn, o_conv],
        weights=[(w_in, _col_tiles(off_ga, tn)), (w_in, _col_tiles(off_gb, tn)),
                 (w_attn_out, 0), (w_conv_out, 0)],
        outs=[(d, BF16)], **common)
    x1, = _ws_call(
        _residual_body, name="mix_out", tn=tn, n_tiles=d // tn, acts=[merged],
        weights=[(w_out, 0)], rows=[gate1], tiles=[x.reshape(m, d)],
        outs=[(d, F32)], **common)
    h2 = _prenorm(x1.reshape(b, s, d), g_ffn, scale2, shift2).reshape(m, d)
    act, = _ws_call(
        _swiglu_body, name="ffn_in", tn=tn, n_tiles=d_ff // tn, acts=[h2],
        weights=[(w_ffn_in, 0), (w_ffn_in, _col_tiles(d_ff, tn))],
        outs=[(d_ff, BF16)], **common)
    x2, = _ws_call(
        _residual_body, name="ffn_out", tn=tn, n_tiles=d // tn, acts=[act],
        weights=[(w_ffn_out, 0)], rows=[gate2], tiles=[x1],
        outs=[(d, F32)], **common)
    return x2.reshape(b, s, d), k, v, conv_state


def kernel(x_prompt, x_sample, cache_k, cache_v, state_conv, page_table, c_prompt, c_sample,
           g_mix, g_ffn, g_final, w_mod, b_mod, w_in, b_sb, conv_w, w_attn_out, w_conv_out,
           w_out, w_ffn_in, w_ffn_out):
    depth = w_in.shape[0]
    bp, sp, d = x_prompt.shape
    db, ds, _ = x_sample.shape
    assert ds == 1 and db <= SAMPLE_ROWS
    n_heads = b_sb.shape[1]
    d_attn = n_heads * HEAD_DIM
    d_conv = conv_w.shape[2]
    n_pool, page = cache_k.shape[1], cache_k.shape[2]
    pad = SAMPLE_ROWS - db

    def pad_rows(a):
        return jnp.pad(a, ((0, pad), (0, 0)))

    xp = x_prompt
    xs = pad_rows(x_sample.reshape(db, d)).reshape(1, SAMPLE_ROWS, d)
    c_all = jnp.concatenate([c_prompt, c_sample], axis=0)
    c_all = jnp.pad(c_all, ((0, -c_all.shape[0] % SAMPLE_ROWS), (0, 0)))
    g_final2 = g_final.reshape(1, d)
    outs = {k: [] for k in ("kp", "vp", "cp", "ks", "vs", "cs")}

    for l in range(depth):
        mod, = _ws_call(
            _mod_body, name="mod", m=c_all.shape[0], tm=c_all.shape[0], tn=1024,
            n_tiles=N_MOD * d // 1024, tiles_per_batch=1, acts=[c_all],
            weights=[(w_mod[l], 0)], rows=[b_mod[l].reshape(1, 1, N_MOD * d)],
            outs=[(N_MOD * d, F32)])
        mods_p = [mod[:bp, i * d:(i + 1) * d].reshape(bp, 1, d) for i in range(N_MOD)]
        mods_s = [pad_rows(mod[bp:bp + db, i * d:(i + 1) * d]).reshape(1, SAMPLE_ROWS, d)
                  for i in range(N_MOD)]
        lw = (g_mix[l].reshape(1, d), g_ffn[l].reshape(1, d), w_in[l], w_attn_out[l],
              w_conv_out[l], w_out[l], w_ffn_in[l], w_ffn_out[l])
        cw = jnp.pad(conv_w[l], ((0, SUBLANES - CONV_W), (0, 0)))
        bias_smem = b_sb[l].reshape(1, n_heads)
        bias_rows = jnp.broadcast_to(b_sb[l].reshape(n_heads, 1), (n_heads, page))

        def attend_p(q, k, v):
            shp = (bp, sp, d_attn)
            return _attn_prompt(q.reshape(shp), k.reshape(shp), v.reshape(shp),
                                bias_smem).reshape(bp * sp, d_attn)

        def conv_p(h, conv_weights, tnc, common):
            tm = common["tm"]
            tpb = common["tiles_per_batch"]
            state_spec = pl.BlockSpec((None, CONV_W - 1, tnc), lambda n, i: (i // tpb, 0, n))
            cw_spec_rows = cw.reshape(1, SUBLANES, d_conv)
            return _ws_call(
                functools.partial(_conv_prompt_body, tpb=tpb), name="conv_prompt", tn=tnc,
                n_tiles=d_conv // tnc, acts=[h], weights=conv_weights,
                rows=[jnp.broadcast_to(cw_spec_rows, (bp, SUBLANES, d_conv))],
                outs=[(d_conv, BF16)],
                extra_outs=[((bp, CONV_W - 1, d_conv), F32, state_spec)],
                extra_scratch=[pltpu.VMEM((tm + SUBLANES, tnc), F32)], **common)

        xp, kp, vp, csp = _layer(xp, mods_p, lw, attend_p, conv_p, tm=512)

        ck, cv = cache_k[l], cache_v[l]

        def attend_s(q, k, v):
            q3 = q[:db].astype(F32).reshape(db, n_heads, HEAD_DIM)
            o = _attn_decode(q3, ck, cv, page_table, bias_rows)
            return pad_rows(o.reshape(db, d_attn)).astype(BF16)

        p0 = pad_rows(state_conv[l][:, 0, :])
        p1 = pad_rows(state_conv[l][:, 1, :])

        def conv_s(h, conv_weights, tnc, common):
            o_conv, u = _ws_call(
                _conv_sample_body, name="conv_sample", tn=tnc, n_tiles=d_conv // tnc,
                acts=[h], weights=conv_weights, rows=[cw.reshape(1, SUBLANES, d_conv)],
                tiles=[p0, p1], outs=[(d_conv, BF16), (d_conv, F32)], **common)
            return o_conv, jnp.stack([p1[:db], u[:db]], axis=1)

        xs, ks, vs, css = _layer(xs, mods_s, lw, attend_s, conv_s, tm=SAMPLE_ROWS)

        outs["kp"].append(kp.reshape(bp, sp, n_heads, HEAD_DIM))
        outs["vp"].append(vp.reshape(bp, sp, n_heads, HEAD_DIM))
        outs["cp"].append(csp)
        outs["ks"].append(ks[:db].reshape(db, 1, n_heads, HEAD_DIM))
        outs["vs"].append(vs[:db].reshape(db, 1, n_heads, HEAD_DIM))
        outs["cs"].append(css)

    y_prompt = _finalnorm(xp, g_final2)
    y_sample = _finalnorm(xs, g_final2)[0, :db].reshape(db, 1, d)
    return (y_prompt, y_sample, jnp.stack(outs["kp"]), jnp.stack(outs["vp"]),
            jnp.stack(outs["cp"]), jnp.stack(outs["ks"]), jnp.stack(outs["vs"]),
            jnp.stack(outs["cs"]))
```

```python
import functools

import jax
import jax.numpy as jnp
from jax import lax
from jax.experimental import pallas as pl
from jax.experimental.pallas import tpu as pltpu

F32 = jnp.float32
BF16 = jnp.bfloat16

EPS = 1e-6
N_MOD = 6
CONV_W = 3
HEAD_DIM = 128
LANES = 128
SUBLANES = 8
SAMPLE_ROWS = 16
VMEM_LIMIT = 56 * 1024 * 1024
CAST_ROWS = 256


def _params(n_axes):
    return pltpu.CompilerParams(
        dimension_semantics=("arbitrary",) * n_axes,
        vmem_limit_bytes=VMEM_LIMIT)


LOG2E = 1.4426950408889634


def _softplus2(z2):
    return jnp.maximum(z2, 0.0) + jnp.log2(1.0 + jnp.exp2(-jnp.abs(z2)))


def _sigmoid(z):
    return 1.0 / (1.0 + jnp.exp(-z))


def _split_bf16(x):
    hi = x.astype(BF16)
    lo = (x - hi.astype(F32)).astype(BF16)
    return hi, lo


def _rms(x):
    return x * lax.rsqrt(jnp.mean(x * x, axis=-1, keepdims=True) + EPS)


def _prenorm_kernel(x_ref, g_ref, scale_ref, shift_ref, h_ref):
    h = _rms(x_ref[...]) * g_ref[...]
    h_ref[...] = (h * (1.0 + scale_ref[...]) + shift_ref[...]).astype(h_ref.dtype)


def _finalnorm_kernel(x_ref, g_ref, y_ref):
    y_ref[...] = _rms(x_ref[...]) * g_ref[...]


def _row_tile(s):
    return min(s, 512)


def _prenorm(x, g, scale, shift):
    b, s, d = x.shape
    tm = _row_tile(s)
    r = scale.shape[1]
    mod_spec = pl.BlockSpec((None, r, d), lambda i, j: (i, 0, 0))
    return pl.pallas_call(
        _prenorm_kernel,
        grid=(b, s // tm),
        in_specs=[pl.BlockSpec((None, tm, d), lambda i, j: (i, j, 0)),
                  pl.BlockSpec((1, d), lambda i, j: (0, 0)),
                  mod_spec, mod_spec],
        out_specs=pl.BlockSpec((None, tm, d), lambda i, j: (i, j, 0)),
        out_shape=jax.ShapeDtypeStruct((b, s, d), BF16),
        compiler_params=_params(2),
        name="prenorm",
    )(x, g, scale, shift)


def _finalnorm(x, g):
    b, s, d = x.shape
    tm = _row_tile(s)
    return pl.pallas_call(
        _finalnorm_kernel,
        grid=(b, s // tm),
        in_specs=[pl.BlockSpec((None, tm, d), lambda i, j: (i, j, 0)),
                  pl.BlockSpec((1, d), lambda i, j: (0, 0))],
        out_specs=pl.BlockSpec((None, tm, d), lambda i, j: (i, j, 0)),
        out_shape=jax.ShapeDtypeStruct((b, s, d), F32),
        compiler_params=_params(2),
        name="finalnorm",
    )(x, g)


def _cast_weight(wb_ref, w_ref):
    def body(i, carry):
        r = pl.multiple_of(i * CAST_ROWS, CAST_ROWS)
        wb_ref[pl.ds(r, CAST_ROWS), :] = w_ref[pl.ds(r, CAST_ROWS), :].astype(BF16)
        return carry
    lax.fori_loop(0, w_ref.shape[0] // CAST_ROWS, body, 0)


def _ws_kernel(*refs, body, side_body, counts):
    refs = iter(refs)
    acts, ws, others, side_acts, side_others, outs, side_outs = (
        [next(refs) for _ in range(n)] for n in counts)
    scratch = list(refs)
    wbs, rest = scratch[:len(ws)], scratch[len(ws):]

    @pl.when(pl.program_id(1) == 0)
    def _():
        for wb, w in zip(wbs, ws):
            _cast_weight(wb, w)
        if side_body is not None:
            side_body(side_acts, wbs, side_others, side_outs, rest)

    body(acts, wbs, others, outs, rest)


def _ws_call(body, *, name, m, tm, tn, n_tiles, tiles_per_batch, acts, weights,
             rows=(), tiles=(), outs=(), extra_outs=(), extra_scratch=(), side=None):
    tpb = tiles_per_batch
    in_specs, args = [], []
    for a in acts:
        in_specs.append(pl.BlockSpec((tm, a.shape[1]), lambda n, i: (i, 0)))
        args.append(a)
    for w, off in weights:
        in_specs.append(pl.BlockSpec((w.shape[0], tn), lambda n, i, off=off: (0, off + n)))
        args.append(w)
    for r in rows:
        in_specs.append(pl.BlockSpec((None, r.shape[1], tn), lambda n, i: (i // tpb, 0, n)))
        args.append(r)
    for t in tiles:
        in_specs.append(pl.BlockSpec((tm, tn), lambda n, i: (i, n)))
        args.append(t)
    n_main_in = len(args)
    out_specs = [pl.BlockSpec((tm, tn), lambda n, i: (i, n)) for _ in outs]
    out_shape = [jax.ShapeDtypeStruct((m, cols), dt) for cols, dt in outs]
    for shape, dt, spec in extra_outs:
        out_specs.append(spec)
        out_shape.append(jax.ShapeDtypeStruct(shape, dt))
    n_main_out = len(out_shape)
    side = side or dict(body=None, acts=(), rows=(), tiles=(), outs=())
    for a in side["acts"]:
        in_specs.append(pl.BlockSpec(a.shape, lambda n, i: (0, 0)))
        args.append(a)
    for r in side["rows"]:
        in_specs.append(pl.BlockSpec((None, r.shape[1], tn), lambda n, i: (0, 0, n)))
        args.append(r)
    for t in side["tiles"]:
        in_specs.append(pl.BlockSpec((t.shape[0], tn), lambda n, i: (0, n)))
        args.append(t)
    for cols, dt in side["outs"]:
        side_rows = side["acts"][0].shape[0]
        out_specs.append(pl.BlockSpec((side_rows, tn), lambda n, i: (0, n)))
        out_shape.append(jax.ShapeDtypeStruct((side_rows, cols), dt))
    counts = (len(acts), len(weights), n_main_in - len(acts) - len(weights),
              len(side["acts"]), len(side["rows"]) + len(side["tiles"]),
              n_main_out, len(side["outs"]))
    scratch = [pltpu.VMEM((w.shape[0], tn), BF16) for w, _ in weights] + list(extra_scratch)
    kern = functools.partial(_ws_kernel, body=body, side_body=side["body"], counts=counts)
    return pl.pallas_call(
        kern,
        grid=(n_tiles, m // tm),
        in_specs=in_specs,
        out_specs=out_specs,
        out_shape=out_shape,
        scratch_shapes=scratch,
        compiler_params=_params(2),
        name=name,
    )(*args)


def _dot(a, wb_ref):
    return jnp.dot(a, wb_ref[...], preferred_element_type=F32)


def _mod_body(acts, wbs, others, outs, scratch):
    c = acts[0][...]
    a = (c * _sigmoid(c)).astype(BF16)
    outs[0][...] = _dot(a, wbs[0]) + others[0][...]


def _qkv_body(acts, wbs, others, outs, scratch):
    a = acts[0][...]
    outs[0][...] = _dot(a, wbs[0]).astype(BF16)
    outs[1][...] = _dot(a, wbs[1])
    outs[2][...] = _dot(a, wbs[2])


def _conv_prompt_body(acts, wbs, others, outs, scratch, *, tpb):
    cw_ref, = others
    o_ref, state_ref = outs
    ubuf, = scratch
    a = acts[0][...]
    tm = a.shape[0]
    pos = pl.program_id(1) % tpb

    @pl.when(pos == 0)
    def _():
        ubuf[0:SUBLANES, :] = jnp.zeros((SUBLANES, ubuf.shape[1]), F32)

    u = _dot(a, wbs[1]) * _dot(a, wbs[2])
    ubuf[SUBLANES:, :] = u
    cw = cw_ref[...]
    conv = (cw[0:1, :] * ubuf[SUBLANES - 2:SUBLANES - 2 + tm, :]
            + cw[1:2, :] * ubuf[SUBLANES - 1:SUBLANES - 1 + tm, :]
            + cw[2:3, :] * u)
    o_ref[...] = (_dot(a, wbs[0]) * conv).astype(o_ref.dtype)
    ubuf[0:SUBLANES, :] = u[tm - SUBLANES:, :]

    @pl.when(pos == tpb - 1)
    def _():
        state_ref[...] = u[tm - (CONV_W - 1):, :]


def _conv_sample_body(acts, wbs, others, outs, scratch):
    cw_ref, = others[:1]
    p0_ref, p1_ref = others[1:]
    o_ref, u_ref = outs
    a = acts[0][...]
    u = _dot(a, wbs[1]) * _dot(a, wbs[2])
    cw = cw_ref[...]
    conv = cw[0:1, :] * p0_ref[...] + cw[1:2, :] * p1_ref[...] + cw[2:3, :] * u
    o_ref[...] = (_dot(a, wbs[0]) * conv).astype(o_ref.dtype)
    u_ref[...] = u


def _merge_body(acts, wbs, others, outs, scratch):
    h = acts[0][...]
    o_attn = acts[1][...]
    o_conv = acts[2][...]
    ga = _sigmoid(_dot(h, wbs[0]))
    gb = _sigmoid(_dot(h, wbs[1]))
    merged = ga * _dot(o_attn, wbs[2]) + gb * _dot(o_conv, wbs[3])
    outs[0][...] = merged.astype(outs[0].dtype)


def _residual_body(acts, wbs, others, outs, scratch):
    gate_ref, x_ref = others
    outs[0][...] = x_ref[...] + gate_ref[...] * _dot(acts[0][...], wbs[0])


def _swiglu_body(acts, wbs, others, outs, scratch):
    a = acts[0][...]
    up = _dot(a, wbs[0])
    outs[0][...] = (up * _sigmoid(up) * _dot(a, wbs[1])).astype(outs[0].dtype)


def _attn_prompt_kernel(bias_ref, q_ref, k_ref, v_ref, o_ref, kb_ref, vb_ref, tri_ref, *, tq, hps):
    group = pl.program_id(1)
    qi = pl.program_id(2)
    row = lax.broadcasted_iota(jnp.int32, (tq, tq), 0)
    col = lax.broadcasted_iota(jnp.int32, (tq, tq), 1)

    @pl.when(qi == 0)
    def _():
        kb_ref[...] = k_ref[...].astype(BF16)
        vb_ref[...] = v_ref[...].astype(BF16)
        neg_tri = jnp.where(row > col, -1.0, 0.0).astype(BF16)
        tri_ref[...] = jnp.concatenate([neg_tri, neg_tri], axis=0)

    causal = col < row
    scale2 = HEAD_DIM ** -0.5 * LOG2E

    heads = range(hps)
    lanes = [slice(i * HEAD_DIM, (i + 1) * HEAD_DIM) for i in heads]
    bias2 = [bias_ref[0, group * hps + i] * LOG2E for i in heads]

    def blocks(start, runs, diagonal):
        z2 = [lax.dot_general(q_ref[:, lanes[i]], kb_ref[pl.ds(start, tq), lanes[i]],
                              (((1,), (1,)), ((), ())), preferred_element_type=F32)
              * scale2 + bias2[i] for i in heads]
        sp2 = [_softplus2(z) for z in z2]
        drop = [jnp.where(causal, s, 0.0) for s in sp2] if diagonal else sp2
        suffix = [jnp.dot(jnp.concatenate(_split_bf16(d), axis=1), tri_ref[...],
                          preferred_element_type=F32) for d in drop]
        w = [jnp.exp2((z2[i] - sp2[i]) + suffix[i] + runs[i]) for i in heads]
        if diagonal:
            w = [jnp.where(causal, x, 0.0) for x in w]
        pv = [jnp.dot(w[i].astype(BF16), vb_ref[pl.ds(start, tq), lanes[i]],
                      preferred_element_type=F32) for i in heads]
        runs = [runs[i] - jnp.sum(drop[i], axis=1, keepdims=True) for i in heads]
        return pv, runs

    first = pl.multiple_of(qi * tq, tq)
    accs, runs = blocks(first, [jnp.zeros((tq, 1), F32)] * hps, True)

    def body(j, carry):
        accs, runs = carry
        pv, runs = blocks(pl.multiple_of((qi - 1 - j) * tq, tq), runs, False)
        return tuple(a + p for a, p in zip(accs, pv)), tuple(runs)

    accs, _ = lax.fori_loop(0, qi, body, (tuple(accs), tuple(runs)))
    for i in heads:
        o_ref[:, lanes[i]] = accs[i].astype(o_ref.dtype)


def _attn_prompt(q, k, v, bias, *, tq=256, hps=4):
    b, s, d = q.shape
    n_heads = d // HEAD_DIM
    width = hps * HEAD_DIM
    kv_spec = pl.BlockSpec((None, s, width), lambda i, g, j: (i, 0, g))
    q_spec = pl.BlockSpec((None, tq, width), lambda i, g, j: (i, j, g))
    return pl.pallas_call(
        functools.partial(_attn_prompt_kernel, tq=tq, hps=hps),
        grid=(b, n_heads // hps, s // tq),
        in_specs=[pl.BlockSpec(memory_space=pltpu.SMEM), q_spec, kv_spec, kv_spec],
        out_specs=q_spec,
        out_shape=jax.ShapeDtypeStruct((b, s, d), BF16),
        scratch_shapes=[pltpu.VMEM((s, width), BF16),
                        pltpu.VMEM((s, width), BF16),
                        pltpu.VMEM((2 * tq, tq), BF16)],
        compiler_params=_params(3),
        name="attn_prompt",
    )(bias, q, k, v)


DECODE_GROUP = 4
DECODE_SLOTS = 3


def _attn_decode_kernel(pt_ref, bias_ref, q_ref, ck_ref, cv_ref, o_ref,
                        kbuf, vbuf, sem, tri_ref, acc_ref, run_ref):
    seq = pl.program_id(0)
    n_seq = pl.num_programs(0)
    n_pages = pt_ref.shape[1]
    grp, n_heads, page, _ = kbuf.shape[1:]
    n_groups = n_pages // grp
    total = n_seq * n_groups

    def group_copies(g):
        first_page = n_pages - (g % n_groups + 1) * grp
        slot = g % DECODE_SLOTS
        copies = []
        for i in range(grp):
            page_id = pt_ref[g // n_groups, first_page + i]
            for h in range(n_heads):
                copies.append(pltpu.make_async_copy(
                    ck_ref.at[page_id, :, h, :], kbuf.at[slot, i, h], sem.at[0, slot]))
                copies.append(pltpu.make_async_copy(
                    cv_ref.at[page_id, :, h, :], vbuf.at[slot, i, h], sem.at[1, slot]))
        return copies

    def start(g):
        for c in group_copies(g):
            c.start()

    def wait(g):
        for c in group_copies(g):
            c.wait()

    head_row = lax.broadcasted_iota(jnp.int32, (n_heads, page), 0)

    @pl.when(seq == 0)
    def _():
        for g in range(DECODE_SLOTS - 1):
            start(g)
        pr = lax.broadcasted_iota(jnp.int32, (page, page), 0)
        pc = lax.broadcasted_iota(jnp.int32, (page, page), 1)
        tri_ref[...] = jnp.where(pr > pc, 1.0, 0.0).astype(BF16)

    acc_ref[...] = jnp.zeros(acc_ref.shape, F32)
    run_ref[...] = jnp.zeros(run_ref.shape, F32)
    qb = q_ref[...].astype(BF16)
    bias2 = jnp.concatenate([bias_ref[...]] * grp, axis=0) * LOG2E
    scale2 = HEAD_DIM ** -0.5 * LOG2E
    rows = grp * n_heads * page

    def body(p, carry):
        g = seq * n_groups + p
        wait(g)
        ahead = g + DECODE_SLOTS - 1

        @pl.when(ahead < total)
        def _():
            start(ahead)

        slot = g % DECODE_SLOTS
        kall = kbuf[slot].reshape(rows, HEAD_DIM).astype(BF16)
        vall = vbuf[slot].reshape(rows, HEAD_DIM).astype(BF16)
        cross = lax.dot_general(qb, kall, (((1,), (1,)), ((), ())),
                                preferred_element_type=F32)
        pieces = []
        for i in range(grp):
            s = jnp.zeros((n_heads, page), F32)
            for t in range(n_heads):
                c0 = (i * n_heads + t) * page
                s = s + jnp.where(head_row == t, cross[:, c0:c0 + page], 0.0)
            pieces.append(s)
        z2 = jnp.concatenate(pieces, axis=0) * scale2 + bias2
        sp2 = _softplus2(z2)
        log_keep = -sp2
        hi, lo = _split_bf16(log_keep)
        tri = tri_ref[...]
        suffix = (jnp.dot(hi, tri, preferred_element_type=F32)
                  + jnp.dot(lo, tri, preferred_element_type=F32))
        page_sum = jnp.sum(log_keep, axis=1, keepdims=True)
        tail = run_ref[...]
        later = [None] * grp
        for i in reversed(range(grp)):
            later[i] = tail
            tail = tail + page_sum[i * n_heads:(i + 1) * n_heads]
        run_ref[...] = tail
        w = jnp.exp2((z2 - sp2) + suffix + jnp.concatenate(later, axis=0))
        w_blocks = jnp.concatenate(
            [jnp.where(head_row == t, w[i * n_heads:(i + 1) * n_heads], 0.0)
             for i in range(grp) for t in range(n_heads)], axis=1).astype(BF16)
        acc_ref[...] += jnp.dot(w_blocks, vall, preferred_element_type=F32)
        return carry

    lax.fori_loop(0, n_groups, body, 0)
    o_ref[...] = acc_ref[...]


def _attn_decode(q, cache_k, cache_v, page_table, bias_rows):
    db, n_heads, _ = q.shape
    page = cache_k.shape[1]
    assert page_table.shape[1] % DECODE_GROUP == 0
    assert db * (page_table.shape[1] // DECODE_GROUP) >= DECODE_SLOTS - 1
    seq_spec = pl.BlockSpec((None, n_heads, HEAD_DIM), lambda b, pt: (b, 0, 0))
    slot_shape = (DECODE_SLOTS, DECODE_GROUP, n_heads, page, HEAD_DIM)
    grid_spec = pltpu.PrefetchScalarGridSpec(
        num_scalar_prefetch=1,
        grid=(db,),
        in_specs=[pl.BlockSpec((n_heads, page), lambda b, pt: (0, 0)), seq_spec,
                  pl.BlockSpec(memory_space=pl.ANY), pl.BlockSpec(memory_space=pl.ANY)],
        out_specs=seq_spec,
        scratch_shapes=[pltpu.VMEM(slot_shape, F32),
                        pltpu.VMEM(slot_shape, F32),
                        pltpu.SemaphoreType.DMA((2, DECODE_SLOTS)),
                        pltpu.VMEM((page, page), BF16),
                        pltpu.VMEM((n_heads, HEAD_DIM), F32),
                        pltpu.VMEM((n_heads, 1), F32)])
    return pl.pallas_call(
        _attn_decode_kernel,
        grid_spec=grid_spec,
        out_shape=jax.ShapeDtypeStruct((db, n_heads, HEAD_DIM), F32),
        compiler_params=_params(1),
        name="attn_decode",
    )(page_table, bias_rows, q, cache_k, cache_v)


def _col_tiles(offset, tn):
    assert offset % tn == 0
    return offset // tn


TM_WIDE = 1024
TM_NARROW = 512
TN = 512
TN_CONV = 256


def _layer(xp, xs, mods_p, mods_s, lw, cw, bias_smem, bias_rows, ck, cv, page_table, p0, p1, db):
    (g_mix, g_ffn, w_in, w_attn_out, w_conv_out, w_out, w_ffn_in, w_ffn_out) = lw
    shift1_p, scale1_p, gate1_p, shift2_p, scale2_p, gate2_p = mods_p
    shift1_s, scale1_s, gate1_s, shift2_s, scale2_s, gate2_s = mods_s
    bp, sp, d = xp.shape
    m = bp * sp
    n_heads = bias_smem.shape[1]
    d_attn = w_attn_out.shape[0]
    d_conv = w_conv_out.shape[0]
    d_ff = w_ffn_out.shape[0]
    off_k, off_v = d_attn, 2 * d_attn
    off_cb = 3 * d_attn
    off_cc, off_ch = off_cb + d_conv, off_cb + 2 * d_conv
    off_ga = off_cb + 3 * d_conv
    off_gb = off_ga + d

    def fused(body, name, *, tm, tn, cols, acts, side_acts, weights, outs, rows=(), side_rows=(),
              tiles=(), side_tiles=(), side_body=None, side_outs=None, extra_outs=(),
              extra_scratch=()):
        res = _ws_call(
            body, name=name, m=m, tm=tm, tn=tn, n_tiles=cols // tn, tiles_per_batch=sp // tm,
            acts=acts, weights=weights, rows=rows, tiles=tiles, outs=outs,
            extra_outs=extra_outs, extra_scratch=extra_scratch,
            side=dict(body=side_body or body, acts=side_acts, rows=side_rows, tiles=side_tiles,
                      outs=outs if side_outs is None else side_outs))
        n_main = len(outs) + len(extra_outs)
        return res[:n_main], res[n_main:]

    def pad_rows(a):
        return jnp.pad(a, ((0, SAMPLE_ROWS - a.shape[0]), (0, 0)))

    h_p = _prenorm(xp, g_mix, scale1_p, shift1_p).reshape(m, d)
    h_s = _prenorm(xs, g_mix, scale1_s, shift1_s).reshape(SAMPLE_ROWS, d)
    (q_p, k_p, v_p), (q_s, k_s, v_s) = fused(
        _qkv_body, "qkv", tm=TM_WIDE, tn=TN, cols=d_attn, acts=[h_p], side_acts=[h_s],
        weights=[(w_in, 0), (w_in, _col_tiles(off_k, TN)), (w_in, _col_tiles(off_v, TN))],
        outs=[(d_attn, BF16), (d_attn, F32), (d_attn, F32)])

    shp = (bp, sp, d_attn)
    o_attn_p = _attn_prompt(q_p.reshape(shp), k_p.reshape(shp), v_p.reshape(shp),
                            bias_smem).reshape(m, d_attn)
    q3 = q_s[:db].astype(F32).reshape(db, n_heads, HEAD_DIM)
    o_attn_s = _attn_decode(q3, ck, cv, page_table, bias_rows)
    o_attn_s = pad_rows(o_attn_s.reshape(db, d_attn)).astype(BF16)

    tpb_conv = sp // TM_WIDE
    state_spec = pl.BlockSpec((None, CONV_W - 1, TN_CONV), lambda n, i: (i // tpb_conv, 0, n))
    cw_rows = cw.reshape(1, SUBLANES, d_conv)
    (o_conv_p, state_p), (o_conv_s, u_s) = fused(
        functools.partial(_conv_prompt_body, tpb=tpb_conv), "conv", tm=TM_WIDE, tn=TN_CONV,
        cols=d_conv, acts=[h_p], side_acts=[h_s],
        weights=[(w_in, _col_tiles(off_cb, TN_CONV)), (w_in, _col_tiles(off_cc, TN_CONV)),
                 (w_in, _col_tiles(off_ch, TN_CONV))],
        rows=[jnp.broadcast_to(cw_rows, (bp, SUBLANES, d_conv))], side_rows=[cw_rows],
        side_tiles=[p0, p1], outs=[(d_conv, BF16)],
        extra_outs=[((bp, CONV_W - 1, d_conv), F32, state_spec)],
        extra_scratch=[pltpu.VMEM((TM_WIDE + SUBLANES, TN_CONV), F32)],
        side_body=_conv_sample_body, side_outs=[(d_conv, BF16), (d_conv, F32)])
    state_s = jnp.stack([p1[:db], u_s[:db]], axis=1)

    (merged_p,), (merged_s,) = fused(
        _merge_body, "merge", tm=TM_NARROW, tn=TN, cols=d,
        acts=[h_p, o_attn_p, o_conv_p], side_acts=[h_s, o_attn_s, o_conv_s],
        weights=[(w_in, _col_tiles(off_ga, TN)), (w_in, _col_tiles(off_gb, TN)),
                 (w_attn_out, 0), (w_conv_out, 0)],
        outs=[(d, BF16)])
    (x1_p,), (x1_s,) = fused(
        _residual_body, "mix_out", tm=TM_WIDE, tn=TN, cols=d, acts=[merged_p],
        side_acts=[merged_s], weights=[(w_out, 0)], rows=[gate1_p], side_rows=[gate1_s],
        tiles=[xp.reshape(m, d)], side_tiles=[xs.reshape(SAMPLE_ROWS, d)], outs=[(d, F32)])
    h2_p = _prenorm(x1_p.reshape(bp, sp, d), g_ffn, scale2_p, shift2_p).reshape(m, d)
    h2_s = _prenorm(x1_s.reshape(1, SAMPLE_ROWS, d), g_ffn, scale2_s, shift2_s)
    (act_p,), (act_s,) = fused(
        _swiglu_body, "ffn_in", tm=TM_WIDE, tn=TN, cols=d_ff, acts=[h2_p],
        side_acts=[h2_s.reshape(SAMPLE_ROWS, d)],
        weights=[(w_ffn_in, 0), (w_ffn_in, _col_tiles(d_ff, TN))], outs=[(d_ff, BF16)])
    (x2_p,), (x2_s,) = fused(
        _residual_body, "ffn_out", tm=TM_NARROW, tn=TN, cols=d, acts=[act_p], side_acts=[act_s],
        weights=[(w_ffn_out, 0)], rows=[gate2_p], side_rows=[gate2_s],
        tiles=[x1_p], side_tiles=[x1_s], outs=[(d, F32)])
    return (x2_p.reshape(bp, sp, d), x2_s.reshape(1, SAMPLE_ROWS, d),
            k_p, v_p, state_p, k_s, v_s, state_s)


def kernel(x_prompt, x_sample, cache_k, cache_v, state_conv, page_table, c_prompt, c_sample,
           g_mix, g_ffn, g_final, w_mod, b_mod, w_in, b_sb, conv_w, w_attn_out, w_conv_out,
           w_out, w_ffn_in, w_ffn_out):
    depth = w_in.shape[0]
    bp, sp, d = x_prompt.shape
    db, ds, _ = x_sample.shape
    assert ds == 1 and db <= SAMPLE_ROWS
    n_heads = b_sb.shape[1]
    d_attn = n_heads * HEAD_DIM
    d_conv = conv_w.shape[2]
    n_pool, page = cache_k.shape[1], cache_k.shape[2]
    pad = SAMPLE_ROWS - db

    def pad_rows(a):
        return jnp.pad(a, ((0, pad), (0, 0)))

    xp = x_prompt
    xs = pad_rows(x_sample.reshape(db, d)).reshape(1, SAMPLE_ROWS, d)
    c_all = jnp.concatenate([c_prompt, c_sample], axis=0)
    c_all = jnp.pad(c_all, ((0, -c_all.shape[0] % SAMPLE_ROWS), (0, 0)))
    g_final2 = g_final.reshape(1, d)
    outs = {k: [] for k in ("kp", "vp", "cp", "ks", "vs", "cs")}

    for l in range(depth):
        mod, = _ws_call(
            _mod_body, name="mod", m=c_all.shape[0], tm=c_all.shape[0], tn=1024,
            n_tiles=N_MOD * d // 1024, tiles_per_batch=1, acts=[c_all],
            weights=[(w_mod[l], 0)], rows=[b_mod[l].reshape(1, 1, N_MOD * d)],
            outs=[(N_MOD * d, F32)])
        mods_p = [mod[:bp, i * d:(i + 1) * d].reshape(bp, 1, d) for i in range(N_MOD)]
        mods_s = [pad_rows(mod[bp:bp + db, i * d:(i + 1) * d]).reshape(1, SAMPLE_ROWS, d)
                  for i in range(N_MOD)]
        lw = (g_mix[l].reshape(1, d), g_ffn[l].reshape(1, d), w_in[l], w_attn_out[l],
              w_conv_out[l], w_out[l], w_ffn_in[l], w_ffn_out[l])
        cw = jnp.pad(conv_w[l], ((0, SUBLANES - CONV_W), (0, 0)))
        bias_smem = b_sb[l].reshape(1, n_heads)
        bias_rows = jnp.broadcast_to(b_sb[l].reshape(n_heads, 1), (n_heads, page))
        p0 = pad_rows(state_conv[l][:, 0, :])
        p1 = pad_rows(state_conv[l][:, 1, :])
        xp, xs, kp, vp, csp, ks, vs, css = _layer(
            xp, xs, mods_p, mods_s, lw, cw, bias_smem, bias_rows, cache_k[l], cache_v[l],
            page_table, p0, p1, db)

        outs["kp"].append(kp.reshape(bp, sp, n_heads, HEAD_DIM))
        outs["vp"].append(vp.reshape(bp, sp, n_heads, HEAD_DIM))
        outs["cp"].append(csp)
        outs["ks"].append(ks[:db].reshape(db, 1, n_heads, HEAD_DIM))
        outs["vs"].append(vs[:db].reshape(db, 1, n_heads, HEAD_DIM))
        outs["cs"].append(css)

    y_prompt = _finalnorm(xp, g_final2)
    y_sample = _finalnorm(xs, g_final2)[0, :db].reshape(db, 1, d)
    return (y_prompt, y_sample, jnp.stack(outs["kp"]), jnp.stack(outs["vp"]),
            jnp.stack(outs["cp"]), jnp.stack(outs["ks"]), jnp.stack(outs["vs"]),
            jnp.stack(outs["cs"]))
```

```python
import functools

import jax
import jax.numpy as jnp
from jax import lax
from jax.experimental import pallas as pl
from jax.experimental.pallas import tpu as pltpu

F32 = jnp.float32
BF16 = jnp.bfloat16

EPS = 1e-6
N_MOD = 6
CONV_W = 3
HEAD_DIM = 128
LANES = 128
SUBLANES = 8
SAMPLE_ROWS = 16
VMEM_LIMIT = 56 * 1024 * 1024
CAST_ROWS = 256


def _params(n_axes):
    return pltpu.CompilerParams(
        dimension_semantics=("arbitrary",) * n_axes,
        vmem_limit_bytes=VMEM_LIMIT)


LOG2E = 1.4426950408889634


def _softplus2(z2):
    return jnp.maximum(z2, 0.0) + jnp.log2(1.0 + jnp.exp2(-jnp.abs(z2)))


def _sigmoid(z):
    return 1.0 / (1.0 + jnp.exp(-z))


def _split_bf16(x):
    hi = x.astype(BF16)
    lo = (x - hi.astype(F32)).astype(BF16)
    return hi, lo


def _neg_tri2(n):
    r = lax.broadcasted_iota(jnp.int32, (n, n), 0)
    c = lax.broadcasted_iota(jnp.int32, (n, n), 1)
    t = jnp.where(r > c, -1.0, 0.0).astype(BF16)
    return jnp.concatenate([t, t], axis=0)


def _rms(x):
    return x * lax.rsqrt(jnp.mean(x * x, axis=-1, keepdims=True) + EPS)


def _prenorm_kernel(x_ref, g_ref, scale_ref, shift_ref, h_ref):
    h = _rms(x_ref[...]) * g_ref[...]
    h_ref[...] = (h * (1.0 + scale_ref[...]) + shift_ref[...]).astype(h_ref.dtype)


def _finalnorm_kernel(x_ref, g_ref, y_ref):
    y_ref[...] = _rms(x_ref[...]) * g_ref[...]


def _row_tile(s):
    return min(s, 512)


def _prenorm(x, g, scale, shift):
    b, s, d = x.shape
    tm = _row_tile(s)
    r = scale.shape[1]
    mod_spec = pl.BlockSpec((None, r, d), lambda i, j: (i, 0, 0))
    return pl.pallas_call(
        _prenorm_kernel,
        grid=(b, s // tm),
        in_specs=[pl.BlockSpec((None, tm, d), lambda i, j: (i, j, 0)),
                  pl.BlockSpec((1, d), lambda i, j: (0, 0)),
                  mod_spec, mod_spec],
        out_specs=pl.BlockSpec((None, tm, d), lambda i, j: (i, j, 0)),
        out_shape=jax.ShapeDtypeStruct((b, s, d), BF16),
        compiler_params=_params(2),
        name="prenorm",
    )(x, g, scale, shift)


def _finalnorm(x, g):
    b, s, d = x.shape
    tm = _row_tile(s)
    return pl.pallas_call(
        _finalnorm_kernel,
        grid=(b, s // tm),
        in_specs=[pl.BlockSpec((None, tm, d), lambda i, j: (i, j, 0)),
                  pl.BlockSpec((1, d), lambda i, j: (0, 0))],
        out_specs=pl.BlockSpec((None, tm, d), lambda i, j: (i, j, 0)),
        out_shape=jax.ShapeDtypeStruct((b, s, d), F32),
        compiler_params=_params(2),
        name="finalnorm",
    )(x, g)


def _cast_weight(wb_ref, w_ref):
    def body(i, carry):
        r = pl.multiple_of(i * CAST_ROWS, CAST_ROWS)
        wb_ref[pl.ds(r, CAST_ROWS), :] = w_ref[pl.ds(r, CAST_ROWS), :].astype(BF16)
        return carry
    lax.fori_loop(0, w_ref.shape[0] // CAST_ROWS, body, 0)


def _ws_kernel(*refs, body, side_body, counts):
    refs = iter(refs)
    acts, ws, others, side_acts, side_others, outs, side_outs = (
        [next(refs) for _ in range(n)] for n in counts)
    scratch = list(refs)
    wbs, rest = scratch[:len(ws)], scratch[len(ws):]

    @pl.when(pl.program_id(1) == 0)
    def _():
        for wb, w in zip(wbs, ws):
            _cast_weight(wb, w)
        if side_body is not None:
            side_body(side_acts, wbs, side_others, side_outs, rest)

    body(acts, wbs, others, outs, rest)


def _ws_call(body, *, name, m, tm, tn, n_tiles, tiles_per_batch, acts, weights,
             rows=(), tiles=(), outs=(), extra_outs=(), extra_scratch=(), side=None):
    tpb = tiles_per_batch
    in_specs, args = [], []
    for a in acts:
        in_specs.append(pl.BlockSpec((tm, a.shape[1]), lambda n, i: (i, 0)))
        args.append(a)
    for w, off in weights:
        in_specs.append(pl.BlockSpec((w.shape[0], tn), lambda n, i, off=off: (0, off + n)))
        args.append(w)
    for r in rows:
        in_specs.append(pl.BlockSpec((None, r.shape[1], tn), lambda n, i: (i // tpb, 0, n)))
        args.append(r)
    for t in tiles:
        in_specs.append(pl.BlockSpec((tm, tn), lambda n, i: (i, n)))
        args.append(t)
    n_main_in = len(args)
    out_specs = [pl.BlockSpec((tm, tn), lambda n, i: (i, n)) for _ in outs]
    out_shape = [jax.ShapeDtypeStruct((m, cols), dt) for cols, dt in outs]
    for shape, dt, spec in extra_outs:
        out_specs.append(spec)
        out_shape.append(jax.ShapeDtypeStruct(shape, dt))
    n_main_out = len(out_shape)
    side = side or dict(body=None, acts=(), rows=(), tiles=(), outs=())
    for a in side["acts"]:
        in_specs.append(pl.BlockSpec(a.shape, lambda n, i: (0, 0)))
        args.append(a)
    for r in side["rows"]:
        in_specs.append(pl.BlockSpec((None, r.shape[1], tn), lambda n, i: (0, 0, n)))
        args.append(r)
    for t in side["tiles"]:
        in_specs.append(pl.BlockSpec((t.shape[0], tn), lambda n, i: (0, n)))
        args.append(t)
    for cols, dt in side["outs"]:
        side_rows = side["acts"][0].shape[0]
        out_specs.append(pl.BlockSpec((side_rows, tn), lambda n, i: (0, n)))
        out_shape.append(jax.ShapeDtypeStruct((side_rows, cols), dt))
    counts = (len(acts), len(weights), n_main_in - len(acts) - len(weights),
              len(side["acts"]), len(side["rows"]) + len(side["tiles"]),
              n_main_out, len(side["outs"]))
    scratch = [pltpu.VMEM((w.shape[0], tn), BF16) for w, _ in weights] + list(extra_scratch)
    kern = functools.partial(_ws_kernel, body=body, side_body=side["body"], counts=counts)
    return pl.pallas_call(
        kern,
        grid=(n_tiles, m // tm),
        in_specs=in_specs,
        out_specs=out_specs,
        out_shape=out_shape,
        scratch_shapes=scratch,
        compiler_params=_params(2),
        name=name,
    )(*args)


def _dot(a, wb_ref):
    return jnp.dot(a, wb_ref[...], preferred_element_type=F32)


def _mod_body(acts, wbs, others, outs, scratch):
    c = acts[0][...]
    a = (c * _sigmoid(c)).astype(BF16)
    outs[0][...] = _dot(a, wbs[0]) + others[0][...]


def _qkv_body(acts, wbs, others, outs, scratch):
    a = acts[0][...]
    outs[0][...] = _dot(a, wbs[0]).astype(BF16)
    outs[1][...] = _dot(a, wbs[1])
    outs[2][...] = _dot(a, wbs[2])


def _conv_prompt_body(acts, wbs, others, outs, scratch, *, tpb):
    cw_ref, = others
    o_ref, state_ref = outs
    ubuf, = scratch
    a = acts[0][...]
    tm = a.shape[0]
    pos = pl.program_id(1) % tpb

    @pl.when(pos == 0)
    def _():
        ubuf[0:SUBLANES, :] = jnp.zeros((SUBLANES, ubuf.shape[1]), F32)

    u = _dot(a, wbs[1]) * _dot(a, wbs[2])
    ubuf[SUBLANES:, :] = u
    cw = cw_ref[...]
    conv = (cw[0:1, :] * ubuf[SUBLANES - 2:SUBLANES - 2 + tm, :]
            + cw[1:2, :] * ubuf[SUBLANES - 1:SUBLANES - 1 + tm, :]
            + cw[2:3, :] * u)
    o_ref[...] = (_dot(a, wbs[0]) * conv).astype(o_ref.dtype)
    ubuf[0:SUBLANES, :] = u[tm - SUBLANES:, :]

    @pl.when(pos == tpb - 1)
    def _():
        state_ref[...] = u[tm - (CONV_W - 1):, :]


def _conv_sample_body(acts, wbs, others, outs, scratch):
    cw_ref, = others[:1]
    p0_ref, p1_ref = others[1:]
    o_ref, u_ref = outs
    a = acts[0][...]
    u = _dot(a, wbs[1]) * _dot(a, wbs[2])
    cw = cw_ref[...]
    conv = cw[0:1, :] * p0_ref[...] + cw[1:2, :] * p1_ref[...] + cw[2:3, :] * u
    o_ref[...] = (_dot(a, wbs[0]) * conv).astype(o_ref.dtype)
    u_ref[...] = u


def _merge_body(acts, wbs, others, outs, scratch):
    h = acts[0][...]
    o_attn = acts[1][...]
    o_conv = acts[2][...]
    ga = _sigmoid(_dot(h, wbs[0]))
    gb = _sigmoid(_dot(h, wbs[1]))
    merged = ga * _dot(o_attn, wbs[2]) + gb * _dot(o_conv, wbs[3])
    outs[0][...] = merged.astype(outs[0].dtype)


def _residual_body(acts, wbs, others, outs, scratch):
    gate_ref, x_ref = others
    outs[0][...] = x_ref[...] + gate_ref[...] * _dot(acts[0][...], wbs[0])


def _swiglu_body(acts, wbs, others, outs, scratch):
    a = acts[0][...]
    up = _dot(a, wbs[0])
    outs[0][...] = (up * _sigmoid(up) * _dot(a, wbs[1])).astype(outs[0].dtype)


ATTN_BLOCK = 256
ATTN_HEADS = 4
PAGE_UNIT = 2
PAGE_SLOTS = 4


def _attn_kernel(pt_ref, bias_ref, q_ref, k_ref, v_ref, sbias_ref, sq_ref, ck_ref, cv_ref,
                 o_ref, so_ref,
                 kb_ref, vb_ref, tri_ref, kbuf, vbuf, sem, stri_ref, sacc_ref, srun_ref,
                 *, tq, hps):
    lin = pl.program_id(0) * pl.num_programs(1) + pl.program_id(1)
    group = pl.program_id(1)
    qi = pl.program_id(2)
    n_q = pl.num_programs(2)
    row = lax.broadcasted_iota(jnp.int32, (tq, tq), 0)
    col = lax.broadcasted_iota(jnp.int32, (tq, tq), 1)
    causal = col < row
    scale2 = HEAD_DIM ** -0.5 * LOG2E

    n_seq, n_pages = pt_ref.shape
    unit, n_heads, page, _ = kbuf.shape[1:]
    units_per_seq = n_pages // unit
    n_units = n_seq * units_per_seq
    unit_rows = unit * n_heads * page
    head_row = lax.broadcasted_iota(jnp.int32, (n_heads, page), 0)

    def unit_copies(u):
        first_page = n_pages - (u % units_per_seq + 1) * unit
        slot = u % PAGE_SLOTS
        copies = []
        for i in range(unit):
            page_id = pt_ref[u // units_per_seq, first_page + i]
            for h in range(n_heads):
                copies.append(pltpu.make_async_copy(
                    ck_ref.at[page_id, :, h, :], kbuf.at[slot, i, h], sem.at[0, slot]))
                copies.append(pltpu.make_async_copy(
                    cv_ref.at[page_id, :, h, :], vbuf.at[slot, i, h], sem.at[1, slot]))
        return copies

    def start(u):
        for c in unit_copies(u):
            c.start()

    def wait(u):
        for c in unit_copies(u):
            c.wait()

    @pl.when((lin == 0) & (qi == 0))
    def _():
        for u in range(PAGE_SLOTS - 1):
            start(u)
        stri_ref[...] = _neg_tri2(page)

    @pl.when(qi == 0)
    def _():
        kb_ref[...] = k_ref[...].astype(BF16)
        vb_ref[...] = v_ref[...].astype(BF16)
        tri_ref[...] = _neg_tri2(tq)

    def unit_begin(t):
        @pl.when(t < n_units)
        def _():
            wait(t)
            ahead = t + PAGE_SLOTS - 1

            @pl.when(ahead < n_units)
            def _():
                start(ahead)

            @pl.when(t % units_per_seq == 0)
            def _():
                sacc_ref[...] = jnp.zeros(sacc_ref.shape, F32)
                srun_ref[...] = jnp.zeros(srun_ref.shape, F32)

    def unit_end(t):
        @pl.when((t < n_units) & (t % units_per_seq == units_per_seq - 1))
        def _():
            so_ref[t // units_per_seq] = sacc_ref[...]

    heads = range(hps)
    lanes = [slice(i * HEAD_DIM, (i + 1) * HEAD_DIM) for i in heads]
    bias2 = [bias_ref[0, group * hps + i] * LOG2E for i in heads]
    sbias2 = jnp.concatenate([sbias_ref[...]] * unit, axis=0) * LOG2E

    def iteration(block, t, runs, diagonal):
        start_row = pl.multiple_of(block * tq, tq)
        live = t < n_units
        slot = t % PAGE_SLOTS
        seq = jnp.minimum(t // units_per_seq, n_seq - 1)

        z2 = [lax.dot_general(q_ref[:, lanes[i]], kb_ref[pl.ds(start_row, tq), lanes[i]],
                              (((1,), (1,)), ((), ())), preferred_element_type=F32)
              * scale2 + bias2[i] for i in heads]
        kall = kbuf[slot].reshape(unit_rows, HEAD_DIM).astype(BF16)
        cross = lax.dot_general(sq_ref[seq].astype(BF16), kall, (((1,), (1,)), ((), ())),
                                preferred_element_type=F32)

        log_beta, drop, split = [], [], []
        for i in heads:
            sp2 = _softplus2(z2[i])
            log_beta.append(z2[i] - sp2)
            drop.append(jnp.where(causal, sp2, 0.0) if diagonal else sp2)
            split.append(jnp.concatenate(_split_bf16(drop[i]), axis=1))
        pieces = []
        for i in range(unit):
            s = jnp.zeros((n_heads, page), F32)
            for h in range(n_heads):
                c0 = (i * n_heads + h) * page
                s = s + jnp.where(head_row == h, cross[:, c0:c0 + page], 0.0)
            pieces.append(s)
        sz2 = jnp.concatenate(pieces, axis=0) * scale2 + sbias2
        ssp2 = _softplus2(sz2)
        ssplit = jnp.concatenate(_split_bf16(ssp2), axis=1)

        suffix = [jnp.dot(s, tri_ref[...], preferred_element_type=F32) for s in split]
        ssuffix = jnp.dot(ssplit, stri_ref[...], preferred_element_type=F32)

        w = [jnp.exp2(log_beta[i] + suffix[i] + runs[i]) for i in heads]
        if diagonal:
            w = [jnp.where(causal, x, 0.0) for x in w]
        page_sum = jnp.sum(ssp2, axis=1, keepdims=True)
        tail = srun_ref[...]
        later = [None] * unit
        for i in reversed(range(unit)):
            later[i] = tail
            tail = tail - page_sum[i * n_heads:(i + 1) * n_heads]
        srun_ref[...] = tail
        sw = jnp.exp2((sz2 - ssp2) + ssuffix + jnp.concatenate(later, axis=0))
        sw = jnp.where(live, sw, 0.0)
        sw_blocks = jnp.concatenate(
            [jnp.where(head_row == h, sw[i * n_heads:(i + 1) * n_heads], 0.0)
             for i in range(unit) for h in range(n_heads)], axis=1).astype(BF16)

        pv = [jnp.dot(w[i].astype(BF16), vb_ref[pl.ds(start_row, tq), lanes[i]],
                      preferred_element_type=F32) for i in heads]
        vall = vbuf[slot].reshape(unit_rows, HEAD_DIM).astype(BF16)
        sacc_ref[...] += jnp.dot(sw_blocks, vall, preferred_element_type=F32)
        runs = [runs[i] - jnp.sum(drop[i], axis=1, keepdims=True) for i in heads]
        return pv, runs

    t0 = lin * (n_q * (n_q + 1) // 2) + qi * (qi + 1) // 2
    unit_begin(t0)
    accs, runs = iteration(qi, t0, [jnp.zeros((tq, 1), F32)] * hps, True)
    unit_end(t0)

    def body(j, carry):
        accs, runs = carry
        t = t0 + 1 + j
        unit_begin(t)
        pv, runs = iteration(qi - 1 - j, t, runs, False)
        unit_end(t)
        return tuple(a + p for a, p in zip(accs, pv)), tuple(runs)

    accs, _ = lax.fori_loop(0, qi, body, (tuple(accs), tuple(runs)))
    for i in heads:
        o_ref[:, lanes[i]] = accs[i].astype(o_ref.dtype)


def _attention(q, k, v, bias, sq, cache_k, cache_v, page_table, sbias_rows):
    b, s, d = q.shape
    tq, hps = ATTN_BLOCK, ATTN_HEADS
    n_heads = d // HEAD_DIM
    width = hps * HEAD_DIM
    db = sq.shape[0]
    page = cache_k.shape[1]
    n_pages = page_table.shape[1]
    n_q = s // tq
    iterations = b * (n_heads // hps) * (n_q * (n_q + 1) // 2)
    assert n_pages % PAGE_UNIT == 0 and s % tq == 0 and n_heads % hps == 0
    n_units = db * (n_pages // PAGE_UNIT)
    assert PAGE_SLOTS - 1 <= n_units <= iterations
    kv_spec = pl.BlockSpec((None, s, width), lambda i, g, j, pt: (i, 0, g))
    q_spec = pl.BlockSpec((None, tq, width), lambda i, g, j, pt: (i, j, g))
    whole = lambda shape: pl.BlockSpec(shape, lambda i, g, j, pt: (0,) * len(shape))
    slot_shape = (PAGE_SLOTS, PAGE_UNIT, n_heads, page, HEAD_DIM)
    grid_spec = pltpu.PrefetchScalarGridSpec(
        num_scalar_prefetch=1,
        grid=(b, n_heads // hps, n_q),
        in_specs=[pl.BlockSpec(memory_space=pltpu.SMEM), q_spec, kv_spec, kv_spec,
                  whole((n_heads, page)), whole((db, n_heads, HEAD_DIM)),
                  pl.BlockSpec(memory_space=pl.ANY), pl.BlockSpec(memory_space=pl.ANY)],
        out_specs=[q_spec, whole((db, n_heads, HEAD_DIM))],
        scratch_shapes=[pltpu.VMEM((s, width), BF16),
                        pltpu.VMEM((s, width), BF16),
                        pltpu.VMEM((2 * tq, tq), BF16),
                        pltpu.VMEM(slot_shape, F32),
                        pltpu.VMEM(slot_shape, F32),
                        pltpu.SemaphoreType.DMA((2, PAGE_SLOTS)),
                        pltpu.VMEM((2 * page, page), BF16),
                        pltpu.VMEM((n_heads, HEAD_DIM), F32),
                        pltpu.VMEM((n_heads, 1), F32)])
    return pl.pallas_call(
        functools.partial(_attn_kernel, tq=tq, hps=hps),
        grid_spec=grid_spec,
        out_shape=[jax.ShapeDtypeStruct((b, s, d), BF16),
                   jax.ShapeDtypeStruct((db, n_heads, HEAD_DIM), F32)],
        compiler_params=_params(3),
        name="attention",
    )(page_table, bias, q, k, v, sbias_rows, sq, cache_k, cache_v)


def _col_tiles(offset, tn):
    assert offset % tn == 0
    return offset // tn


TM_WIDE = 1024
TM_NARROW = 512
TN = 512
TN_WIDE = 1024
TN_CONV = 256


def _layer(xp, xs, mods_p, mods_s, lw, cw, bias_smem, bias_rows, ck, cv, page_table, p0, p1, db):
    (g_mix, g_ffn, w_in, w_attn_out, w_conv_out, w_out, w_ffn_in, w_ffn_out) = lw
    shift1_p, scale1_p, gate1_p, shift2_p, scale2_p, gate2_p = mods_p
    shift1_s, scale1_s, gate1_s, shift2_s, scale2_s, gate2_s = mods_s
    bp, sp, d = xp.shape
    m = bp * sp
    n_heads = bias_smem.shape[1]
    d_attn = w_attn_out.shape[0]
    d_conv = w_conv_out.shape[0]
    d_ff = w_ffn_out.shape[0]
    off_k, off_v = d_attn, 2 * d_attn
    off_cb = 3 * d_attn
    off_cc, off_ch = off_cb + d_conv, off_cb + 2 * d_conv
    off_ga = off_cb + 3 * d_conv
    off_gb = off_ga + d

    def fused(body, name, *, tm, tn, cols, acts, side_acts, weights, outs, rows=(), side_rows=(),
              tiles=(), side_tiles=(), side_body=None, side_outs=None, extra_outs=(),
              extra_scratch=()):
        res = _ws_call(
            body, name=name, m=m, tm=tm, tn=tn, n_tiles=cols // tn, tiles_per_batch=sp // tm,
            acts=acts, weights=weights, rows=rows, tiles=tiles, outs=outs,
            extra_outs=extra_outs, extra_scratch=extra_scratch,
            side=dict(body=side_body or body, acts=side_acts, rows=side_rows, tiles=side_tiles,
                      outs=outs if side_outs is None else side_outs))
        n_main = len(outs) + len(extra_outs)
        return res[:n_main], res[n_main:]

    def pad_rows(a):
        return jnp.pad(a, ((0, SAMPLE_ROWS - a.shape[0]), (0, 0)))

    h_p = _prenorm(xp, g_mix, scale1_p, shift1_p).reshape(m, d)
    h_s = _prenorm(xs, g_mix, scale1_s, shift1_s).reshape(SAMPLE_ROWS, d)
    (q_p, k_p, v_p), (q_s, k_s, v_s) = fused(
        _qkv_body, "qkv", tm=TM_WIDE, tn=TN, cols=d_attn, acts=[h_p], side_acts=[h_s],
        weights=[(w_in, 0), (w_in, _col_tiles(off_k, TN)), (w_in, _col_tiles(off_v, TN))],
        outs=[(d_attn, BF16), (d_attn, F32), (d_attn, F32)])

    shp = (bp, sp, d_attn)
    q3 = q_s[:db].astype(F32).reshape(db, n_heads, HEAD_DIM)
    o_attn_p, o_attn_s = _attention(q_p.reshape(shp), k_p.reshape(shp), v_p.reshape(shp),
                                    bias_smem, q3, ck, cv, page_table, bias_rows)
    o_attn_p = o_attn_p.reshape(m, d_attn)
    o_attn_s = pad_rows(o_attn_s.reshape(db, d_attn)).astype(BF16)

    tpb_conv = sp // TM_WIDE
    state_spec = pl.BlockSpec((None, CONV_W - 1, TN_CONV), lambda n, i: (i // tpb_conv, 0, n))
    cw_rows = cw.reshape(1, SUBLANES, d_conv)
    (o_conv_p, state_p), (o_conv_s, u_s) = fused(
        functools.partial(_conv_prompt_body, tpb=tpb_conv), "conv", tm=TM_WIDE, tn=TN_CONV,
        cols=d_conv, acts=[h_p], side_acts=[h_s],
        weights=[(w_in, _col_tiles(off_cb, TN_CONV)), (w_in, _col_tiles(off_cc, TN_CONV)),
                 (w_in, _col_tiles(off_ch, TN_CONV))],
        rows=[jnp.broadcast_to(cw_rows, (bp, SUBLANES, d_conv))], side_rows=[cw_rows],
        side_tiles=[p0, p1], outs=[(d_conv, BF16)],
        extra_outs=[((bp, CONV_W - 1, d_conv), F32, state_spec)],
        extra_scratch=[pltpu.VMEM((TM_WIDE + SUBLANES, TN_CONV), F32)],
        side_body=_conv_sample_body, side_outs=[(d_conv, BF16), (d_conv, F32)])
    state_s = jnp.stack([p1[:db], u_s[:db]], axis=1)

    (merged_p,), (merged_s,) = fused(
        _merge_body, "merge", tm=TM_NARROW, tn=TN, cols=d,
        acts=[h_p, o_attn_p, o_conv_p], side_acts=[h_s, o_attn_s, o_conv_s],
        weights=[(w_in, _col_tiles(off_ga, TN)), (w_in, _col_tiles(off_gb, TN)),
                 (w_attn_out, 0), (w_conv_out, 0)],
        outs=[(d, BF16)])
    (x1_p,), (x1_s,) = fused(
        _residual_body, "mix_out", tm=TM_WIDE, tn=TN_WIDE, cols=d, acts=[merged_p],
        side_acts=[merged_s], weights=[(w_out, 0)], rows=[gate1_p], side_rows=[gate1_s],
        tiles=[xp.reshape(m, d)], side_tiles=[xs.reshape(SAMPLE_ROWS, d)], outs=[(d, F32)])
    h2_p = _prenorm(x1_p.reshape(bp, sp, d), g_ffn, scale2_p, shift2_p).reshape(m, d)
    h2_s = _prenorm(x1_s.reshape(1, SAMPLE_ROWS, d), g_ffn, scale2_s, shift2_s)
    (act_p,), (act_s,) = fused(
        _swiglu_body, "ffn_in", tm=TM_WIDE, tn=TN, cols=d_ff, acts=[h2_p],
        side_acts=[h2_s.reshape(SAMPLE_ROWS, d)],
        weights=[(w_ffn_in, 0), (w_ffn_in, _col_tiles(d_ff, TN))], outs=[(d_ff, BF16)])
    (x2_p,), (x2_s,) = fused(
        _residual_body, "ffn_out", tm=TM_NARROW, tn=TN, cols=d, acts=[act_p], side_acts=[act_s],
        weights=[(w_ffn_out, 0)], rows=[gate2_p], side_rows=[gate2_s],
        tiles=[x1_p], side_tiles=[x1_s], outs=[(d, F32)])
    return (x2_p.reshape(bp, sp, d), x2_s.reshape(1, SAMPLE_ROWS, d),
            k_p, v_p, state_p, k_s, v_s, state_s)


def kernel(x_prompt, x_sample, cache_k, cache_v, state_conv, page_table, c_prompt, c_sample,
           g_mix, g_ffn, g_final, w_mod, b_mod, w_in, b_sb, conv_w, w_attn_out, w_conv_out,
           w_out, w_ffn_in, w_ffn_out):
    depth = w_in.shape[0]
    bp, sp, d = x_prompt.shape
    db, ds, _ = x_sample.shape
    assert ds == 1 and db <= SAMPLE_ROWS
    n_heads = b_sb.shape[1]
    page = cache_k.shape[2]
    pad = SAMPLE_ROWS - db

    def pad_rows(a):
        return jnp.pad(a, ((0, pad), (0, 0)))

    xp = x_prompt
    xs = pad_rows(x_sample.reshape(db, d)).reshape(1, SAMPLE_ROWS, d)
    c_all = jnp.concatenate([c_prompt, c_sample], axis=0)
    c_all = jnp.pad(c_all, ((0, -c_all.shape[0] % SAMPLE_ROWS), (0, 0)))
    g_final2 = g_final.reshape(1, d)
    outs = {k: [] for k in ("kp", "vp", "cp", "ks", "vs", "cs")}

    for l in range(depth):
        mod, = _ws_call(
            _mod_body, name="mod", m=c_all.shape[0], tm=c_all.shape[0], tn=1024,
            n_tiles=N_MOD * d // 1024, tiles_per_batch=1, acts=[c_all],
            weights=[(w_mod[l], 0)], rows=[b_mod[l].reshape(1, 1, N_MOD * d)],
            outs=[(N_MOD * d, F32)])
        mods_p = [mod[:bp, i * d:(i + 1) * d].reshape(bp, 1, d) for i in range(N_MOD)]
        mods_s = [pad_rows(mod[bp:bp + db, i * d:(i + 1) * d]).reshape(1, SAMPLE_ROWS, d)
                  for i in range(N_MOD)]
        lw = (g_mix[l].reshape(1, d), g_ffn[l].reshape(1, d), w_in[l], w_attn_out[l],
              w_conv_out[l], w_out[l], w_ffn_in[l], w_ffn_out[l])
        cw = jnp.pad(conv_w[l], ((0, SUBLANES - CONV_W), (0, 0)))
        bias_smem = b_sb[l].reshape(1, n_heads)
        bias_rows = jnp.broadcast_to(b_sb[l].reshape(n_heads, 1), (n_heads, page))
        p0 = pad_rows(state_conv[l][:, 0, :])
        p1 = pad_rows(state_conv[l][:, 1, :])
        xp, xs, kp, vp, csp, ks, vs, css = _layer(
            xp, xs, mods_p, mods_s, lw, cw, bias_smem, bias_rows, cache_k[l], cache_v[l],
            page_table, p0, p1, db)

        outs["kp"].append(kp.reshape(bp, sp, n_heads, HEAD_DIM))
        outs["vp"].append(vp.reshape(bp, sp, n_heads, HEAD_DIM))
        outs["cp"].append(csp)
        outs["ks"].append(ks[:db].reshape(db, 1, n_heads, HEAD_DIM))
        outs["vs"].append(vs[:db].reshape(db, 1, n_heads, HEAD_DIM))
        outs["cs"].append(css)

    y_prompt = _finalnorm(xp, g_final2)
    y_sample = _finalnorm(xs, g_final2)[0, :db].reshape(db, 1, d)
    return (y_prompt, y_sample, jnp.stack(outs["kp"]), jnp.stack(outs["vp"]),
            jnp.stack(outs["cp"]), jnp.stack(outs["ks"]), jnp.stack(outs["vs"]),
            jnp.stack(outs["cs"]))
```

```python
import functools

import jax
import jax.numpy as jnp
from jax import lax
from jax.experimental import pallas as pl
from jax.experimental.pallas import tpu as pltpu

F32 = jnp.float32
BF16 = jnp.bfloat16

EPS = 1e-6
N_MOD = 6
CONV_W = 3
HEAD_DIM = 128
LANES = 128
SUBLANES = 8
SAMPLE_ROWS = 16
VMEM_LIMIT = 56 * 1024 * 1024
CAST_ROWS = 256


def _params(n_axes):
    return pltpu.CompilerParams(
        dimension_semantics=("arbitrary",) * n_axes,
        vmem_limit_bytes=VMEM_LIMIT)


LOG2E = 1.4426950408889634


def _softplus2(z2):
    return jnp.maximum(z2, 0.0) + jnp.log2(1.0 + jnp.exp2(-jnp.abs(z2)))


def _sigmoid(z):
    return 1.0 / (1.0 + jnp.exp(-z))


def _split_bf16(x):
    hi = x.astype(BF16)
    lo = (x - hi.astype(F32)).astype(BF16)
    return hi, lo


def _neg_tri2(n):
    r = lax.broadcasted_iota(jnp.int32, (n, n), 0)
    c = lax.broadcasted_iota(jnp.int32, (n, n), 1)
    t = jnp.where(r > c, -1.0, 0.0).astype(BF16)
    return jnp.concatenate([t, t], axis=0)


def _rms(x):
    return x * lax.rsqrt(jnp.mean(x * x, axis=-1, keepdims=True) + EPS)


def _prenorm_kernel(x_ref, g_ref, scale_ref, shift_ref, h_ref):
    h = _rms(x_ref[...]) * g_ref[...]
    h_ref[...] = (h * (1.0 + scale_ref[...]) + shift_ref[...]).astype(h_ref.dtype)


def _finalnorm_kernel(x_ref, g_ref, y_ref):
    y_ref[...] = _rms(x_ref[...]) * g_ref[...]


def _row_tile(s):
    return min(s, 512)


def _prenorm(x, g, scale, shift):
    b, s, d = x.shape
    tm = _row_tile(s)
    r = scale.shape[1]
    mod_spec = pl.BlockSpec((None, r, d), lambda i, j: (i, 0, 0))
    return pl.pallas_call(
        _prenorm_kernel,
        grid=(b, s // tm),
        in_specs=[pl.BlockSpec((None, tm, d), lambda i, j: (i, j, 0)),
                  pl.BlockSpec((1, d), lambda i, j: (0, 0)),
                  mod_spec, mod_spec],
        out_specs=pl.BlockSpec((None, tm, d), lambda i, j: (i, j, 0)),
        out_shape=jax.ShapeDtypeStruct((b, s, d), BF16),
        compiler_params=_params(2),
        name="prenorm",
    )(x, g, scale, shift)


def _finalnorm(x, g):
    b, s, d = x.shape
    tm = _row_tile(s)
    return pl.pallas_call(
        _finalnorm_kernel,
        grid=(b, s // tm),
        in_specs=[pl.BlockSpec((None, tm, d), lambda i, j: (i, j, 0)),
                  pl.BlockSpec((1, d), lambda i, j: (0, 0))],
        out_specs=pl.BlockSpec((None, tm, d), lambda i, j: (i, j, 0)),
        out_shape=jax.ShapeDtypeStruct((b, s, d), F32),
        compiler_params=_params(2),
        name="finalnorm",
    )(x, g)


def _cast_weight(wb_ref, w_ref):
    def body(i, carry):
        r = pl.multiple_of(i * CAST_ROWS, CAST_ROWS)
        wb_ref[pl.ds(r, CAST_ROWS), :] = w_ref[pl.ds(r, CAST_ROWS), :].astype(BF16)
        return carry
    lax.fori_loop(0, w_ref.shape[0] // CAST_ROWS, body, 0)


def _ws_kernel(*refs, body, side_body, counts):
    refs = iter(refs)
    acts, ws, others, side_acts, side_others, outs, side_outs = (
        [next(refs) for _ in range(n)] for n in counts)
    scratch = list(refs)
    wbs, rest = scratch[:len(ws)], scratch[len(ws):]

    @pl.when(pl.program_id(1) == 0)
    def _():
        for wb, w in zip(wbs, ws):
            _cast_weight(wb, w)
        if side_body is not None:
            side_body(side_acts, wbs, side_others, side_outs, rest)

    body(acts, wbs, others, outs, rest)


def _ws_call(body, *, name, m, tm, tn, n_tiles, tiles_per_batch, acts, weights,
             rows=(), tiles=(), outs=(), extra_outs=(), extra_scratch=(), side=None):
    tpb = tiles_per_batch
    in_specs, args = [], []
    for a in acts:
        in_specs.append(pl.BlockSpec((tm, a.shape[1]), lambda n, i: (i, 0)))
        args.append(a)
    for w, off in weights:
        in_specs.append(pl.BlockSpec((w.shape[0], tn), lambda n, i, off=off: (0, off + n)))
        args.append(w)
    for r in rows:
        in_specs.append(pl.BlockSpec((None, r.shape[1], tn), lambda n, i: (i // tpb, 0, n)))
        args.append(r)
    for t in tiles:
        in_specs.append(pl.BlockSpec((tm, tn), lambda n, i: (i, n)))
        args.append(t)
    n_main_in = len(args)
    out_specs = [pl.BlockSpec((tm, tn), lambda n, i: (i, n)) for _ in outs]
    out_shape = [jax.ShapeDtypeStruct((m, cols), dt) for cols, dt in outs]
    for shape, dt, spec in extra_outs:
        out_specs.append(spec)
        out_shape.append(jax.ShapeDtypeStruct(shape, dt))
    n_main_out = len(out_shape)
    side = side or dict(body=None, acts=(), rows=(), tiles=(), outs=())
    for a in side["acts"]:
        in_specs.append(pl.BlockSpec(a.shape, lambda n, i: (0, 0)))
        args.append(a)
    for r in side["rows"]:
        in_specs.append(pl.BlockSpec((None, r.shape[1], tn), lambda n, i: (0, 0, n)))
        args.append(r)
    for t in side["tiles"]:
        in_specs.append(pl.BlockSpec((t.shape[0], tn), lambda n, i: (0, n)))
        args.append(t)
    for cols, dt in side["outs"]:
        side_rows = side["acts"][0].shape[0]
        out_specs.append(pl.BlockSpec((side_rows, tn), lambda n, i: (0, n)))
        out_shape.append(jax.ShapeDtypeStruct((side_rows, cols), dt))
    counts = (len(acts), len(weights), n_main_in - len(acts) - len(weights),
              len(side["acts"]), len(side["rows"]) + len(side["tiles"]),
              n_main_out, len(side["outs"]))
    scratch = [pltpu.VMEM((w.shape[0], tn), BF16) for w, _ in weights] + list(extra_scratch)
    kern = functools.partial(_ws_kernel, body=body, side_body=side["body"], counts=counts)
    return pl.pallas_call(
        kern,
        grid=(n_tiles, m // tm),
        in_specs=in_specs,
        out_specs=out_specs,
        out_shape=out_shape,
        scratch_shapes=scratch,
        compiler_params=_params(2),
        name=name,
    )(*args)


def _dot(a, wb_ref):
    return jnp.dot(a, wb_ref[...], preferred_element_type=F32)


def _mod_body(acts, wbs, others, outs, scratch):
    c = acts[0][...]
    a = (c * _sigmoid(c)).astype(BF16)
    outs[0][...] = _dot(a, wbs[0]) + others[0][...]


def _qkv_body(acts, wbs, others, outs, scratch):
    a = acts[0][...]
    outs[0][...] = _dot(a, wbs[0]).astype(BF16)
    outs[1][...] = _dot(a, wbs[1])
    outs[2][...] = _dot(a, wbs[2])


def _conv_prompt_body(acts, wbs, others, outs, scratch, *, tpb):
    cw_ref, = others
    o_ref, state_ref = outs
    ubuf, = scratch
    a = acts[0][...]
    tm = a.shape[0]
    pos = pl.program_id(1) % tpb

    @pl.when(pos == 0)
    def _():
        ubuf[0:SUBLANES, :] = jnp.zeros((SUBLANES, ubuf.shape[1]), F32)

    u = _dot(a, wbs[1]) * _dot(a, wbs[2])
    ubuf[SUBLANES:, :] = u
    cw = cw_ref[...]
    conv = (cw[0:1, :] * ubuf[SUBLANES - 2:SUBLANES - 2 + tm, :]
            + cw[1:2, :] * ubuf[SUBLANES - 1:SUBLANES - 1 + tm, :]
            + cw[2:3, :] * u)
    o_ref[...] = (_dot(a, wbs[0]) * conv).astype(o_ref.dtype)
    ubuf[0:SUBLANES, :] = u[tm - SUBLANES:, :]

    @pl.when(pos == tpb - 1)
    def _():
        state_ref[...] = u[tm - (CONV_W - 1):, :]


def _conv_sample_body(acts, wbs, others, outs, scratch):
    cw_ref, = others[:1]
    p0_ref, p1_ref = others[1:]
    o_ref, u_ref = outs
    a = acts[0][...]
    u = _dot(a, wbs[1]) * _dot(a, wbs[2])
    cw = cw_ref[...]
    conv = cw[0:1, :] * p0_ref[...] + cw[1:2, :] * p1_ref[...] + cw[2:3, :] * u
    o_ref[...] = (_dot(a, wbs[0]) * conv).astype(o_ref.dtype)
    u_ref[...] = u


def _merge_body(acts, wbs, others, outs, scratch):
    h = acts[0][...]
    o_attn = acts[1][...]
    o_conv = acts[2][...]
    ga = _sigmoid(_dot(h, wbs[0]))
    gb = _sigmoid(_dot(h, wbs[1]))
    merged = ga * _dot(o_attn, wbs[2]) + gb * _dot(o_conv, wbs[3])
    outs[0][...] = merged.astype(outs[0].dtype)


def _residual_body(acts, wbs, others, outs, scratch):
    gate_ref, x_ref = others
    outs[0][...] = x_ref[...] + gate_ref[...] * _dot(acts[0][...], wbs[0])


def _residual_prenorm_body(acts, wbs, others, outs, scratch):
    gate_ref, g_ref, scale_ref, shift_ref, x_ref = others
    x1 = x_ref[...] + gate_ref[...] * _dot(acts[0][...], wbs[0])
    outs[0][...] = x1
    h = _rms(x1) * g_ref[...]
    outs[1][...] = (h * (1.0 + scale_ref[...]) + shift_ref[...]).astype(outs[1].dtype)


def _swiglu_body(acts, wbs, others, outs, scratch):
    a = acts[0][...]
    up = _dot(a, wbs[0])
    outs[0][...] = (up * _sigmoid(up) * _dot(a, wbs[1])).astype(outs[0].dtype)


ATTN_BLOCK = 256
ATTN_HEADS = 4
PAGE_UNIT = 2
PAGE_SLOTS = 4


def _attn_kernel(pt_ref, bias_ref, q_ref, k_ref, v_ref, sbias_ref, sq_ref, ck_ref, cv_ref,
                 o_ref, so_ref,
                 kb_ref, vb_ref, tri_ref, acc_ref, run_ref,
                 kbuf, vbuf, sem, stri_ref, sacc_ref, srun_ref, *, tq, hps):
    lin = pl.program_id(0) * pl.num_programs(1) + pl.program_id(1)
    group = pl.program_id(1)
    qi = pl.program_id(2)
    n_q = pl.num_programs(2)
    row = lax.broadcasted_iota(jnp.int32, (tq, tq), 0)
    col = lax.broadcasted_iota(jnp.int32, (tq, tq), 1)
    causal = col < row
    scale2 = HEAD_DIM ** -0.5 * LOG2E

    n_seq, n_pages = pt_ref.shape
    unit, n_heads, page, _ = kbuf.shape[1:]
    units_per_seq = n_pages // unit
    n_units = n_seq * units_per_seq
    unit_rows = unit * n_heads * page
    head_row = lax.broadcasted_iota(jnp.int32, (n_heads, page), 0)

    def unit_copies(u):
        first_page = n_pages - (u % units_per_seq + 1) * unit
        slot = u % PAGE_SLOTS
        copies = []
        for i in range(unit):
            page_id = pt_ref[u // units_per_seq, first_page + i]
            for h in range(n_heads):
                copies.append(pltpu.make_async_copy(
                    ck_ref.at[page_id, :, h, :], kbuf.at[slot, i, h], sem.at[slot]))
                copies.append(pltpu.make_async_copy(
                    cv_ref.at[page_id, :, h, :], vbuf.at[slot, i, h], sem.at[slot]))
        return copies

    def start(u):
        for c in unit_copies(u):
            c.start()

    def wait(u):
        for c in unit_copies(u):
            c.wait()

    @pl.when((lin == 0) & (qi == 0))
    def _():
        for u in range(PAGE_SLOTS - 1):
            start(u)
        stri_ref[...] = _neg_tri2(page)
        sacc_ref[...] = jnp.zeros(sacc_ref.shape, F32)
        srun_ref[...] = jnp.zeros(srun_ref.shape, F32)

    @pl.when(qi == 0)
    def _():
        kb_ref[...] = k_ref[...].astype(BF16)
        vb_ref[...] = v_ref[...].astype(BF16)
        tri_ref[...] = _neg_tri2(tq)

    def maybe(cond, guarded, fn):
        if guarded:
            pl.when(cond)(fn)
        else:
            fn()

    heads = range(hps)
    lanes = [slice(i * HEAD_DIM, (i + 1) * HEAD_DIM) for i in heads]
    bias2 = [bias_ref[0, group * hps + i] * LOG2E for i in heads]
    sbias2 = jnp.concatenate([sbias_ref[...]] * unit, axis=0) * LOG2E

    def iteration(block, t, diagonal, guarded):
        start_row = pl.multiple_of(block * tq, tq)
        live = t < n_units
        slot = t % PAGE_SLOTS
        seq = jnp.minimum(t // units_per_seq, n_seq - 1)
        maybe(live, guarded, lambda: wait(t))

        z2 = [lax.dot_general(q_ref[:, lanes[i]], kb_ref[pl.ds(start_row, tq), lanes[i]],
                              (((1,), (1,)), ((), ())), preferred_element_type=F32)
              * scale2 + bias2[i] for i in heads]
        kall = kbuf[slot].reshape(unit_rows, HEAD_DIM).astype(BF16)
        cross = lax.dot_general(sq_ref[seq].astype(BF16), kall, (((1,), (1,)), ((), ())),
                                preferred_element_type=F32)

        log_beta, drop, split = [], [], []
        for i in heads:
            sp2 = _softplus2(z2[i])
            log_beta.append(z2[i] - sp2)
            drop.append(jnp.where(causal, sp2, 0.0) if diagonal else sp2)
            split.append(jnp.concatenate(_split_bf16(drop[i]), axis=1))
        pieces = []
        for i in range(unit):
            s = jnp.zeros((n_heads, page), F32)
            for h in range(n_heads):
                c0 = (i * n_heads + h) * page
                s = s + jnp.where(head_row == h, cross[:, c0:c0 + page], 0.0)
            pieces.append(s)
        sz2 = jnp.concatenate(pieces, axis=0) * scale2 + sbias2
        ssp2 = _softplus2(sz2)
        ssplit = jnp.concatenate(_split_bf16(ssp2), axis=1)

        suffix = [jnp.dot(s, tri_ref[...], preferred_element_type=F32) for s in split]
        ssuffix = jnp.dot(ssplit, stri_ref[...], preferred_element_type=F32)

        if diagonal:
            w = [jnp.where(causal, jnp.exp2(log_beta[i] + suffix[i]), 0.0) for i in heads]
        else:
            w = [jnp.exp2(log_beta[i] + suffix[i] + run_ref[i]) for i in heads]
        page_sum = jnp.sum(ssp2, axis=1, keepdims=True)
        fresh = live & (t % units_per_seq == 0)
        tail = jnp.where(fresh, 0.0, srun_ref[...])
        later = [None] * unit
        for i in reversed(range(unit)):
            later[i] = tail
            tail = tail - page_sum[i * n_heads:(i + 1) * n_heads]
        srun_ref[...] = tail
        sw = jnp.exp2((sz2 - ssp2) + ssuffix + jnp.concatenate(later, axis=0))
        sw = jnp.where(live, sw, 0.0)
        sw_blocks = jnp.concatenate(
            [jnp.where(head_row == h, sw[i * n_heads:(i + 1) * n_heads], 0.0)
             for i in range(unit) for h in range(n_heads)], axis=1).astype(BF16)

        pv = [jnp.dot(w[i].astype(BF16), vb_ref[pl.ds(start_row, tq), lanes[i]],
                      preferred_element_type=F32) for i in heads]
        vall = vbuf[slot].reshape(unit_rows, HEAD_DIM).astype(BF16)
        sacc = (jnp.where(fresh, 0.0, sacc_ref[...])
                + jnp.dot(sw_blocks, vall, preferred_element_type=F32))
        sacc_ref[...] = sacc
        so_ref[seq] = sacc
        for i in heads:
            mass = jnp.sum(drop[i], axis=1, keepdims=True)
            if diagonal:
                acc_ref[i] = pv[i]
                run_ref[i] = -mass
            else:
                acc_ref[i] += pv[i]
                run_ref[i] -= mass
        ahead = t + PAGE_SLOTS - 1
        maybe(ahead < n_units, guarded, lambda: start(ahead))

    t0 = lin * (n_q * (n_q + 1) // 2) + qi * (qi + 1) // 2

    def step(guarded):
        iteration(qi, t0, True, guarded)

        def body(j, carry):
            iteration(qi - 1 - j, t0 + 1 + j, False, guarded)
            return carry

        lax.fori_loop(0, qi, body, 0)
        for i in heads:
            o_ref[:, lanes[i]] = acc_ref[i].astype(o_ref.dtype)

    unguarded = t0 + qi + PAGE_SLOTS - 1 < n_units
    pl.when(unguarded)(lambda: step(False))
    pl.when(jnp.logical_not(unguarded))(lambda: step(True))


def _attention(q, k, v, bias, sq, cache_k, cache_v, page_table, sbias_rows):
    b, s, d = q.shape
    tq, hps = ATTN_BLOCK, ATTN_HEADS
    n_heads = d // HEAD_DIM
    width = hps * HEAD_DIM
    db = sq.shape[0]
    page = cache_k.shape[1]
    n_pages = page_table.shape[1]
    n_q = s // tq
    iterations = b * (n_heads // hps) * (n_q * (n_q + 1) // 2)
    assert n_pages % PAGE_UNIT == 0 and s % tq == 0 and n_heads % hps == 0
    n_units = db * (n_pages // PAGE_UNIT)
    assert PAGE_SLOTS - 1 <= n_units <= iterations
    kv_spec = pl.BlockSpec((None, s, width), lambda i, g, j, pt: (i, 0, g))
    q_spec = pl.BlockSpec((None, tq, width), lambda i, g, j, pt: (i, j, g))
    whole = lambda shape: pl.BlockSpec(shape, lambda i, g, j, pt: (0,) * len(shape))
    slot_shape = (PAGE_SLOTS, PAGE_UNIT, n_heads, page, HEAD_DIM)
    grid_spec = pltpu.PrefetchScalarGridSpec(
        num_scalar_prefetch=1,
        grid=(b, n_heads // hps, n_q),
        in_specs=[pl.BlockSpec(memory_space=pltpu.SMEM), q_spec, kv_spec, kv_spec,
                  whole((n_heads, page)), whole((db, n_heads, HEAD_DIM)),
                  pl.BlockSpec(memory_space=pl.ANY), pl.BlockSpec(memory_space=pl.ANY)],
        out_specs=[q_spec, whole((db, n_heads, HEAD_DIM))],
        scratch_shapes=[pltpu.VMEM((s, width), BF16),
                        pltpu.VMEM((s, width), BF16),
                        pltpu.VMEM((2 * tq, tq), BF16),
                        pltpu.VMEM((hps, tq, HEAD_DIM), F32),
                        pltpu.VMEM((hps, tq, 1), F32),
                        pltpu.VMEM(slot_shape, F32),
                        pltpu.VMEM(slot_shape, F32),
                        pltpu.SemaphoreType.DMA((PAGE_SLOTS,)),
                        pltpu.VMEM((2 * page, page), BF16),
                        pltpu.VMEM((n_heads, HEAD_DIM), F32),
                        pltpu.VMEM((n_heads, 1), F32)])
    return pl.pallas_call(
        functools.partial(_attn_kernel, tq=tq, hps=hps),
        grid_spec=grid_spec,
        out_shape=[jax.ShapeDtypeStruct((b, s, d), BF16),
                   jax.ShapeDtypeStruct((db, n_heads, HEAD_DIM), F32)],
        compiler_params=_params(3),
        name="attention",
    )(page_table, bias, q, k, v, sbias_rows, sq, cache_k, cache_v)


def _col_tiles(offset, tn):
    assert offset % tn == 0
    return offset // tn


TM_WIDE = 1024
TM_NARROW = 512
TN = 512
TM_FULL_ROWS = 512
TN_CONV = 256


def _layer(xp, xs, mods_p, mods_s, lw, cw, bias_smem, bias_rows, ck, cv, page_table, p0, p1, db):
    (g_mix, g_ffn, w_in, w_attn_out, w_conv_out, w_out, w_ffn_in, w_ffn_out) = lw
    shift1_p, scale1_p, gate1_p, shift2_p, scale2_p, gate2_p = mods_p
    shift1_s, scale1_s, gate1_s, shift2_s, scale2_s, gate2_s = mods_s
    bp, sp, d = xp.shape
    m = bp * sp
    n_heads = bias_smem.shape[1]
    d_attn = w_attn_out.shape[0]
    d_conv = w_conv_out.shape[0]
    d_ff = w_ffn_out.shape[0]
    off_k, off_v = d_attn, 2 * d_attn
    off_cb = 3 * d_attn
    off_cc, off_ch = off_cb + d_conv, off_cb + 2 * d_conv
    off_ga = off_cb + 3 * d_conv
    off_gb = off_ga + d

    def fused(body, name, *, tm, tn, cols, acts, side_acts, weights, outs, rows=(), side_rows=(),
              tiles=(), side_tiles=(), side_body=None, side_outs=None, extra_outs=(),
              extra_scratch=()):
        res = _ws_call(
            body, name=name, m=m, tm=tm, tn=tn, n_tiles=cols // tn, tiles_per_batch=sp // tm,
            acts=acts, weights=weights, rows=rows, tiles=tiles, outs=outs,
            extra_outs=extra_outs, extra_scratch=extra_scratch,
            side=dict(body=side_body or body, acts=side_acts, rows=side_rows, tiles=side_tiles,
                      outs=outs if side_outs is None else side_outs))
        n_main = len(outs) + len(extra_outs)
        return res[:n_main], res[n_main:]

    def pad_rows(a):
        return jnp.pad(a, ((0, SAMPLE_ROWS - a.shape[0]), (0, 0)))

    h_p = _prenorm(xp, g_mix, scale1_p, shift1_p).reshape(m, d)
    h_s = _prenorm(xs, g_mix, scale1_s, shift1_s).reshape(SAMPLE_ROWS, d)
    (q_p, k_p, v_p), (q_s, k_s, v_s) = fused(
        _qkv_body, "qkv", tm=TM_WIDE, tn=TN, cols=d_attn, acts=[h_p], side_acts=[h_s],
        weights=[(w_in, 0), (w_in, _col_tiles(off_k, TN)), (w_in, _col_tiles(off_v, TN))],
        outs=[(d_attn, BF16), (d_attn, F32), (d_attn, F32)])

    shp = (bp, sp, d_attn)
    q3 = q_s[:db].astype(F32).reshape(db, n_heads, HEAD_DIM)
    o_attn_p, o_attn_s = _attention(q_p.reshape(shp), k_p.reshape(shp), v_p.reshape(shp),
                                    bias_smem, q3, ck, cv, page_table, bias_rows)
    o_attn_p = o_attn_p.reshape(m, d_attn)
    o_attn_s = pad_rows(o_attn_s.reshape(db, d_attn)).astype(BF16)

    tpb_conv = sp // TM_WIDE
    state_spec = pl.BlockSpec((None, CONV_W - 1, TN_CONV), lambda n, i: (i // tpb_conv, 0, n))
    cw_rows = cw.reshape(1, SUBLANES, d_conv)
    (o_conv_p, state_p), (o_conv_s, u_s) = fused(
        functools.partial(_conv_prompt_body, tpb=tpb_conv), "conv", tm=TM_WIDE, tn=TN_CONV,
        cols=d_conv, acts=[h_p], side_acts=[h_s],
        weights=[(w_in, _col_tiles(off_cb, TN_CONV)), (w_in, _col_tiles(off_cc, TN_CONV)),
                 (w_in, _col_tiles(off_ch, TN_CONV))],
        rows=[jnp.broadcast_to(cw_rows, (bp, SUBLANES, d_conv))], side_rows=[cw_rows],
        side_tiles=[p0, p1], outs=[(d_conv, BF16)],
        extra_outs=[((bp, CONV_W - 1, d_conv), F32, state_spec)],
        extra_scratch=[pltpu.VMEM((TM_WIDE + SUBLANES, TN_CONV), F32)],
        side_body=_conv_sample_body, side_outs=[(d_conv, BF16), (d_conv, F32)])
    state_s = jnp.stack([p1[:db], u_s[:db]], axis=1)

    (merged_p,), (merged_s,) = fused(
        _merge_body, "merge", tm=TM_NARROW, tn=TN, cols=d,
        acts=[h_p, o_attn_p, o_conv_p], side_acts=[h_s, o_attn_s, o_conv_s],
        weights=[(w_in, _col_tiles(off_ga, TN)), (w_in, _col_tiles(off_gb, TN)),
                 (w_attn_out, 0), (w_conv_out, 0)],
        outs=[(d, BF16)])
    g_rows = g_ffn.reshape(1, 1, d)
    (x1_p, h2_p), (x1_s, h2_s) = fused(
        _residual_prenorm_body, "mix_out", tm=TM_FULL_ROWS, tn=d, cols=d, acts=[merged_p],
        side_acts=[merged_s], weights=[(w_out, 0)],
        rows=[gate1_p, jnp.broadcast_to(g_rows, (bp, 1, d)), scale2_p, shift2_p],
        side_rows=[gate1_s, g_rows, scale2_s, shift2_s],
        tiles=[xp.reshape(m, d)], side_tiles=[xs.reshape(SAMPLE_ROWS, d)],
        outs=[(d, F32), (d, BF16)])
    (act_p,), (act_s,) = fused(
        _swiglu_body, "ffn_in", tm=TM_WIDE, tn=TN, cols=d_ff, acts=[h2_p],
        side_acts=[h2_s],
        weights=[(w_ffn_in, 0), (w_ffn_in, _col_tiles(d_ff, TN))], outs=[(d_ff, BF16)])
    (x2_p,), (x2_s,) = fused(
        _residual_body, "ffn_out", tm=TM_NARROW, tn=TN, cols=d, acts=[act_p], side_acts=[act_s],
        weights=[(w_ffn_out, 0)], rows=[gate2_p], side_rows=[gate2_s],
        tiles=[x1_p], side_tiles=[x1_s], outs=[(d, F32)])
    return (x2_p.reshape(bp, sp, d), x2_s.reshape(1, SAMPLE_ROWS, d),
            k_p, v_p, state_p, k_s, v_s, state_s)


def kernel(x_prompt, x_sample, cache_k, cache_v, state_conv, page_table, c_prompt, c_sample,
           g_mix, g_ffn, g_final, w_mod, b_mod, w_in, b_sb, conv_w, w_attn_out, w_conv_out,
           w_out, w_ffn_in, w_ffn_out):
    depth = w_in.shape[0]
    bp, sp, d = x_prompt.shape
    db, ds, _ = x_sample.shape
    assert ds == 1 and db <= SAMPLE_ROWS
    n_heads = b_sb.shape[1]
    page = cache_k.shape[2]
    pad = SAMPLE_ROWS - db

    def pad_rows(a):
        return jnp.pad(a, ((0, pad), (0, 0)))

    xp = x_prompt
    xs = pad_rows(x_sample.reshape(db, d)).reshape(1, SAMPLE_ROWS, d)
    c_all = jnp.concatenate([c_prompt, c_sample], axis=0)
    c_all = jnp.pad(c_all, ((0, -c_all.shape[0] % SAMPLE_ROWS), (0, 0)))
    g_final2 = g_final.reshape(1, d)
    outs = {k: [] for k in ("kp", "vp", "cp", "ks", "vs", "cs")}

    for l in range(depth):
        mod, = _ws_call(
            _mod_body, name="mod", m=c_all.shape[0], tm=c_all.shape[0], tn=1024,
            n_tiles=N_MOD * d // 1024, tiles_per_batch=1, acts=[c_all],
            weights=[(w_mod[l], 0)], rows=[b_mod[l].reshape(1, 1, N_MOD * d)],
            outs=[(N_MOD * d, F32)])
        mods_p = [mod[:bp, i * d:(i + 1) * d].reshape(bp, 1, d) for i in range(N_MOD)]
        mods_s = [pad_rows(mod[bp:bp + db, i * d:(i + 1) * d]).reshape(1, SAMPLE_ROWS, d)
                  for i in range(N_MOD)]
        lw = (g_mix[l].reshape(1, d), g_ffn[l].reshape(1, d), w_in[l], w_attn_out[l],
              w_conv_out[l], w_out[l], w_ffn_in[l], w_ffn_out[l])
        cw = jnp.pad(conv_w[l], ((0, SUBLANES - CONV_W), (0, 0)))
        bias_smem = b_sb[l].reshape(1, n_heads)
        bias_rows = jnp.broadcast_to(b_sb[l].reshape(n_heads, 1), (n_heads, page))
        p0 = pad_rows(state_conv[l][:, 0, :])
        p1 = pad_rows(state_conv[l][:, 1, :])
        xp, xs, kp, vp, csp, ks, vs, css = _layer(
            xp, xs, mods_p, mods_s, lw, cw, bias_smem, bias_rows, cache_k[l], cache_v[l],
            page_table, p0, p1, db)

        outs["kp"].append(kp.reshape(bp, sp, n_heads, HEAD_DIM))
        outs["vp"].append(vp.reshape(bp, sp, n_heads, HEAD_DIM))
        outs["cp"].append(csp)
        outs["ks"].append(ks[:db].reshape(db, 1, n_heads, HEAD_DIM))
        outs["vs"].append(vs[:db].reshape(db, 1, n_heads, HEAD_DIM))
        outs["cs"].append(css)

    y_prompt = _finalnorm(xp, g_final2)
    y_sample = _finalnorm(xs, g_final2)[0, :db].reshape(db, 1, d)
    return (y_prompt, y_sample, jnp.stack(outs["kp"]), jnp.stack(outs["vp"]),
            jnp.stack(outs["cp"]), jnp.stack(outs["ks"]), jnp.stack(outs["vs"]),
            jnp.stack(outs["cs"]))
```

```python
import functools

import jax
import jax.numpy as jnp
from jax import lax
from jax.experimental import pallas as pl
from jax.experimental.pallas import tpu as pltpu

F32 = jnp.float32
BF16 = jnp.bfloat16

EPS = 1e-6
N_MOD = 6
CONV_W = 3
HEAD_DIM = 128
LANES = 128
SUBLANES = 8
SAMPLE_ROWS = 16
VMEM_LIMIT = 56 * 1024 * 1024
CAST_ROWS = 256


def _params(n_axes):
    return pltpu.CompilerParams(
        dimension_semantics=("arbitrary",) * n_axes,
        vmem_limit_bytes=VMEM_LIMIT)


LOG2E = 1.4426950408889634


def _softplus2(z2):
    return jnp.maximum(z2, 0.0) + jnp.log2(1.0 + jnp.exp2(-jnp.abs(z2)))


def _sigmoid(z):
    return 1.0 / (1.0 + jnp.exp(-z))


def _neg_tri(n):
    r = lax.broadcasted_iota(jnp.int32, (n, n), 0)
    c = lax.broadcasted_iota(jnp.int32, (n, n), 1)
    return jnp.where(r > c, -1.0, 0.0).astype(BF16)


def _rms(x):
    return x * lax.rsqrt(jnp.mean(x * x, axis=-1, keepdims=True) + EPS)


def _prenorm_kernel(x_ref, g_ref, scale_ref, shift_ref, h_ref):
    h = _rms(x_ref[...]) * g_ref[...]
    h_ref[...] = (h * (1.0 + scale_ref[...]) + shift_ref[...]).astype(h_ref.dtype)


def _finalnorm_kernel(x_ref, g_ref, y_ref):
    y_ref[...] = _rms(x_ref[...]) * g_ref[...]


def _row_tile(s):
    return min(s, 512)


def _prenorm(x, g, scale, shift):
    b, s, d = x.shape
    tm = _row_tile(s)
    r = scale.shape[1]
    mod_spec = pl.BlockSpec((None, r, d), lambda i, j: (i, 0, 0))
    return pl.pallas_call(
        _prenorm_kernel,
        grid=(b, s // tm),
        in_specs=[pl.BlockSpec((None, tm, d), lambda i, j: (i, j, 0)),
                  pl.BlockSpec((1, d), lambda i, j: (0, 0)),
                  mod_spec, mod_spec],
        out_specs=pl.BlockSpec((None, tm, d), lambda i, j: (i, j, 0)),
        out_shape=jax.ShapeDtypeStruct((b, s, d), BF16),
        compiler_params=_params(2),
        name="prenorm",
    )(x, g, scale, shift)


def _finalnorm(x, g):
    b, s, d = x.shape
    tm = _row_tile(s)
    return pl.pallas_call(
        _finalnorm_kernel,
        grid=(b, s // tm),
        in_specs=[pl.BlockSpec((None, tm, d), lambda i, j: (i, j, 0)),
                  pl.BlockSpec((1, d), lambda i, j: (0, 0))],
        out_specs=pl.BlockSpec((None, tm, d), lambda i, j: (i, j, 0)),
        out_shape=jax.ShapeDtypeStruct((b, s, d), F32),
        compiler_params=_params(2),
        name="finalnorm",
    )(x, g)


def _cast_weight(wb_ref, w_ref):
    def body(i, carry):
        r = pl.multiple_of(i * CAST_ROWS, CAST_ROWS)
        wb_ref[pl.ds(r, CAST_ROWS), :] = w_ref[pl.ds(r, CAST_ROWS), :].astype(BF16)
        return carry
    lax.fori_loop(0, w_ref.shape[0] // CAST_ROWS, body, 0)


def _ws_kernel(*refs, body, side_body, counts):
    refs = iter(refs)
    acts, ws, others, side_acts, side_others, outs, side_outs = (
        [next(refs) for _ in range(n)] for n in counts)
    scratch = list(refs)
    wbs, rest = scratch[:len(ws)], scratch[len(ws):]

    @pl.when(pl.program_id(1) == 0)
    def _():
        for wb, w in zip(wbs, ws):
            _cast_weight(wb, w)
        if side_body is not None:
            side_body(side_acts, wbs, side_others, side_outs, rest)

    body(acts, wbs, others, outs, rest)


def _ws_call(body, *, name, m, tm, tn, n_tiles, tiles_per_batch, acts, weights,
             rows=(), tiles=(), outs=(), extra_outs=(), extra_scratch=(), side=None):
    tpb = tiles_per_batch
    in_specs, args = [], []
    for a in acts:
        in_specs.append(pl.BlockSpec((tm, a.shape[1]), lambda n, i: (i, 0)))
        args.append(a)
    for w, off in weights:
        in_specs.append(pl.BlockSpec((w.shape[0], tn), lambda n, i, off=off: (0, off + n)))
        args.append(w)
    for r in rows:
        in_specs.append(pl.BlockSpec((None, r.shape[1], tn), lambda n, i: (i // tpb, 0, n)))
        args.append(r)
    for t in tiles:
        in_specs.append(pl.BlockSpec((tm, tn), lambda n, i: (i, n)))
        args.append(t)
    n_main_in = len(args)
    out_specs = [pl.BlockSpec((tm, tn), lambda n, i: (i, n)) for _ in outs]
    out_shape = [jax.ShapeDtypeStruct((m, cols), dt) for cols, dt in outs]
    for shape, dt, spec in extra_outs:
        out_specs.append(spec)
        out_shape.append(jax.ShapeDtypeStruct(shape, dt))
    n_main_out = len(out_shape)
    side = side or dict(body=None, acts=(), rows=(), tiles=(), outs=())
    for a in side["acts"]:
        in_specs.append(pl.BlockSpec(a.shape, lambda n, i: (0, 0)))
        args.append(a)
    for r in side["rows"]:
        in_specs.append(pl.BlockSpec((None, r.shape[1], tn), lambda n, i: (0, 0, n)))
        args.append(r)
    for t in side["tiles"]:
        in_specs.append(pl.BlockSpec((t.shape[0], tn), lambda n, i: (0, n)))
        args.append(t)
    for cols, dt in side["outs"]:
        side_rows = side["acts"][0].shape[0]
        out_specs.append(pl.BlockSpec((side_rows, tn), lambda n, i: (0, n)))
        out_shape.append(jax.ShapeDtypeStruct((side_rows, cols), dt))
    counts = (len(acts), len(weights), n_main_in - len(acts) - len(weights),
              len(side["acts"]), len(side["rows"]) + len(side["tiles"]),
              n_main_out, len(side["outs"]))
    scratch = [pltpu.VMEM((w.shape[0], tn), BF16) for w, _ in weights] + list(extra_scratch)
    kern = functools.partial(_ws_kernel, body=body, side_body=side["body"], counts=counts)
    return pl.pallas_call(
        kern,
        grid=(n_tiles, m // tm),
        in_specs=in_specs,
        out_specs=out_specs,
        out_shape=out_shape,
        scratch_shapes=scratch,
        compiler_params=_params(2),
        name=name,
    )(*args)


def _dot(a, wb_ref):
    return jnp.dot(a, wb_ref[...], preferred_element_type=F32)


def _mod_body(acts, wbs, others, outs, scratch):
    c = acts[0][...]
    a = (c * _sigmoid(c)).astype(BF16)
    outs[0][...] = _dot(a, wbs[0]) + others[0][...]


def _qkv_body(acts, wbs, others, outs, scratch):
    a = acts[0][...]
    outs[0][...] = _dot(a, wbs[0]).astype(BF16)
    k = _dot(a, wbs[1])
    v = _dot(a, wbs[2])
    outs[1][...] = k
    outs[2][...] = v
    outs[3][...] = k.astype(BF16)
    outs[4][...] = v.astype(BF16)


def _conv_prompt_body(acts, wbs, others, outs, scratch, *, tpb):
    cw_ref, = others
    o_ref, state_ref = outs
    tail_ref, = scratch
    a = acts[0][...]
    tm = a.shape[0]
    pos = pl.program_id(1) % tpb

    @pl.when(pos == 0)
    def _():
        tail_ref[...] = jnp.zeros(tail_ref.shape, F32)

    u = _dot(a, wbs[1]) * _dot(a, wbs[2])
    prev1 = tail_ref[SUBLANES - 1:SUBLANES, :]
    prev2 = tail_ref[SUBLANES - 2:SUBLANES - 1, :]
    row = lax.broadcasted_iota(jnp.int32, u.shape, 0)
    u1 = jnp.where(row == 0, prev1, pltpu.roll(u, 1, 0))
    u2 = jnp.where(row == 0, prev2, jnp.where(row == 1, prev1, pltpu.roll(u, 2, 0)))
    cw = cw_ref[...]
    conv = cw[0:1, :] * u2 + cw[1:2, :] * u1 + cw[2:3, :] * u
    o_ref[...] = (_dot(a, wbs[0]) * conv).astype(o_ref.dtype)
    tail_ref[...] = u[tm - SUBLANES:, :]

    @pl.when(pos == tpb - 1)
    def _():
        state_ref[...] = u[tm - (CONV_W - 1):, :]


def _conv_sample_body(acts, wbs, others, outs, scratch):
    cw_ref, = others[:1]
    p0_ref, p1_ref = others[1:]
    o_ref, u_ref = outs
    a = acts[0][...]
    u = _dot(a, wbs[1]) * _dot(a, wbs[2])
    cw = cw_ref[...]
    conv = cw[0:1, :] * p0_ref[...] + cw[1:2, :] * p1_ref[...] + cw[2:3, :] * u
    o_ref[...] = (_dot(a, wbs[0]) * conv).astype(o_ref.dtype)
    u_ref[...] = u


def _merge_body(acts, wbs, others, outs, scratch):
    h = acts[0][...]
    o_attn = acts[1][...]
    o_conv = acts[2][...]
    ga = _sigmoid(_dot(h, wbs[0]))
    gb = _sigmoid(_dot(h, wbs[1]))
    merged = ga * _dot(o_attn, wbs[2]) + gb * _dot(o_conv, wbs[3])
    outs[0][...] = merged.astype(outs[0].dtype)


def _residual_body(acts, wbs, others, outs, scratch):
    gate_ref, x_ref = others
    outs[0][...] = x_ref[...] + gate_ref[...] * _dot(acts[0][...], wbs[0])


def _residual_prenorm_body(acts, wbs, others, outs, scratch):
    gate_ref, g_ref, scale_ref, shift_ref, x_ref = others
    x1 = x_ref[...] + gate_ref[...] * _dot(acts[0][...], wbs[0])
    outs[0][...] = x1
    h = _rms(x1) * g_ref[...]
    outs[1][...] = (h * (1.0 + scale_ref[...]) + shift_ref[...]).astype(outs[1].dtype)


def _swiglu_body(acts, wbs, others, outs, scratch):
    a = acts[0][...]
    up = _dot(a, wbs[0])
    outs[0][...] = (up * _sigmoid(up) * _dot(a, wbs[1])).astype(outs[0].dtype)


ATTN_BLOCK = 256
ATTN_HEADS = 4
PAGE_UNIT = 2
PAGE_SLOTS = 4


def _attn_kernel(pt_ref, bias_ref, q_ref, k_ref, v_ref, sbias_ref, sq_ref, ck_ref, cv_ref,
                 o_ref, so_ref,
                 tri_ref, acc_ref, run_ref,
                 kbuf, vbuf, sem, stri_ref, sacc_ref, srun_ref, *, tq, hps):
    lin = pl.program_id(0) * pl.num_programs(1) + pl.program_id(1)
    group = pl.program_id(1)
    qi = pl.program_id(2)
    n_q = pl.num_programs(2)
    row = lax.broadcasted_iota(jnp.int32, (tq, tq), 0)
    col = lax.broadcasted_iota(jnp.int32, (tq, tq), 1)
    causal = col < row
    scale2 = HEAD_DIM ** -0.5 * LOG2E

    n_seq, n_pages = pt_ref.shape
    unit, n_heads, page, _ = kbuf.shape[1:]
    units_per_seq = n_pages // unit
    n_units = n_seq * units_per_seq
    unit_rows = unit * n_heads * page
    head_row = lax.broadcasted_iota(jnp.int32, (n_heads, page), 0)

    def unit_copies(u):
        first_page = n_pages - (u % units_per_seq + 1) * unit
        slot = u % PAGE_SLOTS
        copies = []
        for i in range(unit):
            page_id = pt_ref[u // units_per_seq, first_page + i]
            for h in range(n_heads):
                copies.append(pltpu.make_async_copy(
                    ck_ref.at[page_id, :, h, :], kbuf.at[slot, i, h], sem.at[slot]))
                copies.append(pltpu.make_async_copy(
                    cv_ref.at[page_id, :, h, :], vbuf.at[slot, i, h], sem.at[slot]))
        return copies

    def start(u):
        for c in unit_copies(u):
            c.start()

    def wait(u):
        for c in unit_copies(u):
            c.wait()

    @pl.when((lin == 0) & (qi == 0))
    def _():
        for u in range(PAGE_SLOTS - 1):
            start(u)
        stri_ref[...] = _neg_tri(page)
        sacc_ref[...] = jnp.zeros(sacc_ref.shape, F32)
        srun_ref[...] = jnp.zeros(srun_ref.shape, F32)

    @pl.when(qi == 0)
    def _():
        tri_ref[...] = _neg_tri(tq)

    def maybe(cond, guarded, fn):
        if guarded:
            pl.when(cond)(fn)
        else:
            fn()

    heads = range(hps)
    lanes = [slice(i * HEAD_DIM, (i + 1) * HEAD_DIM) for i in heads]
    bias2 = [bias_ref[0, group * hps + i] * LOG2E for i in heads]
    sbias2 = jnp.concatenate([sbias_ref[...]] * unit, axis=0) * LOG2E

    def iteration(block, t, diagonal, guarded):
        start_row = pl.multiple_of(block * tq, tq)
        live = t < n_units
        slot = t % PAGE_SLOTS
        seq = jnp.minimum(t // units_per_seq, n_seq - 1)
        maybe(live, guarded, lambda: wait(t))

        z2 = [lax.dot_general(q_ref[:, lanes[i]], k_ref[pl.ds(start_row, tq), lanes[i]],
                              (((1,), (1,)), ((), ())), preferred_element_type=F32)
              * scale2 + bias2[i] for i in heads]
        kall = kbuf[slot].reshape(unit_rows, HEAD_DIM).astype(BF16)
        cross = lax.dot_general(sq_ref[seq].astype(BF16), kall, (((1,), (1,)), ((), ())),
                                preferred_element_type=F32)

        log_beta, drop, split = [], [], []
        for i in heads:
            sp2 = _softplus2(z2[i])
            log_beta.append(z2[i] - sp2)
            drop.append(jnp.where(causal, sp2, 0.0) if diagonal else sp2)
            split.append(drop[i].astype(BF16))
        pieces = []
        for i in range(unit):
            s = jnp.zeros((n_heads, page), F32)
            for h in range(n_heads):
                c0 = (i * n_heads + h) * page
                s = s + jnp.where(head_row == h, cross[:, c0:c0 + page], 0.0)
            pieces.append(s)
        sz2 = jnp.concatenate(pieces, axis=0) * scale2 + sbias2
        ssp2 = _softplus2(sz2)
        ssplit = ssp2.astype(BF16)

        suffix = [jnp.dot(s, tri_ref[...], preferred_element_type=F32) for s in split]
        ssuffix = jnp.dot(ssplit, stri_ref[...], preferred_element_type=F32)

        if diagonal:
            w = [jnp.where(causal, jnp.exp2(log_beta[i] + suffix[i]), 0.0) for i in heads]
        else:
            w = [jnp.exp2(log_beta[i] + suffix[i] + run_ref[i]) for i in heads]
        page_sum = jnp.sum(ssp2, axis=1, keepdims=True)
        fresh = live & (t % units_per_seq == 0)
        tail = jnp.where(fresh, 0.0, srun_ref[...])
        later = [None] * unit
        for i in reversed(range(unit)):
            later[i] = tail
            tail = tail - page_sum[i * n_heads:(i + 1) * n_heads]
        srun_ref[...] = tail
        sw = jnp.exp2((sz2 - ssp2) + ssuffix + jnp.concatenate(later, axis=0))
        sw = jnp.where(live, sw, 0.0)
        sw_blocks = jnp.concatenate(
            [jnp.where(head_row == h, sw[i * n_heads:(i + 1) * n_heads], 0.0)
             for i in range(unit) for h in range(n_heads)], axis=1).astype(BF16)

        pv = [jnp.dot(w[i].astype(BF16), v_ref[pl.ds(start_row, tq), lanes[i]],
                      preferred_element_type=F32) for i in heads]
        vall = vbuf[slot].reshape(unit_rows, HEAD_DIM).astype(BF16)
        sacc = (jnp.where(fresh, 0.0, sacc_ref[...])
                + jnp.dot(sw_blocks, vall, preferred_element_type=F32))
        sacc_ref[...] = sacc
        so_ref[seq] = sacc
        for i in heads:
            mass = jnp.sum(drop[i], axis=1, keepdims=True)
            if diagonal:
                acc_ref[i] = pv[i]
                run_ref[i] = -mass
            else:
                acc_ref[i] += pv[i]
                run_ref[i] -= mass
        ahead = t + PAGE_SLOTS - 1
        maybe(ahead < n_units, guarded, lambda: start(ahead))

    t0 = lin * (n_q * (n_q + 1) // 2) + qi * (qi + 1) // 2

    def step(guarded):
        iteration(qi, t0, True, guarded)

        def body(j, carry):
            iteration(qi - 1 - j, t0 + 1 + j, False, guarded)
            return carry

        lax.fori_loop(0, qi, body, 0)
        for i in heads:
            o_ref[:, lanes[i]] = acc_ref[i].astype(o_ref.dtype)

    unguarded = t0 + qi + PAGE_SLOTS - 1 < n_units
    pl.when(unguarded)(lambda: step(False))
    pl.when(jnp.logical_not(unguarded))(lambda: step(True))


def _attention(q, k, v, bias, sq, cache_k, cache_v, page_table, sbias_rows):
    b, s, d = q.shape
    tq, hps = ATTN_BLOCK, ATTN_HEADS
    n_heads = d // HEAD_DIM
    width = hps * HEAD_DIM
    db = sq.shape[0]
    page = cache_k.shape[1]
    n_pages = page_table.shape[1]
    n_q = s // tq
    iterations = b * (n_heads // hps) * (n_q * (n_q + 1) // 2)
    assert n_pages % PAGE_UNIT == 0 and s % tq == 0 and n_heads % hps == 0
    n_units = db * (n_pages // PAGE_UNIT)
    assert PAGE_SLOTS - 1 <= n_units <= iterations
    kv_spec = pl.BlockSpec((None, s, width), lambda i, g, j, pt: (i, 0, g))
    q_spec = pl.BlockSpec((None, tq, width), lambda i, g, j, pt: (i, j, g))
    whole = lambda shape: pl.BlockSpec(shape, lambda i, g, j, pt: (0,) * len(shape))
    slot_shape = (PAGE_SLOTS, PAGE_UNIT, n_heads, page, HEAD_DIM)
    grid_spec = pltpu.PrefetchScalarGridSpec(
        num_scalar_prefetch=1,
        grid=(b, n_heads // hps, n_q),
        in_specs=[pl.BlockSpec(memory_space=pltpu.SMEM), q_spec, kv_spec, kv_spec,
                  whole((n_heads, page)), whole((db, n_heads, HEAD_DIM)),
                  pl.BlockSpec(memory_space=pl.ANY), pl.BlockSpec(memory_space=pl.ANY)],
        out_specs=[q_spec, whole((db, n_heads, HEAD_DIM))],
        scratch_shapes=[pltpu.VMEM((tq, tq), BF16),
                        pltpu.VMEM((hps, tq, HEAD_DIM), F32),
                        pltpu.VMEM((hps, tq, 1), F32),
                        pltpu.VMEM(slot_shape, F32),
                        pltpu.VMEM(slot_shape, F32),
                        pltpu.SemaphoreType.DMA((PAGE_SLOTS,)),
                        pltpu.VMEM((page, page), BF16),
                        pltpu.VMEM((n_heads, HEAD_DIM), F32),
                        pltpu.VMEM((n_heads, 1), F32)])
    return pl.pallas_call(
        functools.partial(_attn_kernel, tq=tq, hps=hps),
        grid_spec=grid_spec,
        out_shape=[jax.ShapeDtypeStruct((b, s, d), BF16),
                   jax.ShapeDtypeStruct((db, n_heads, HEAD_DIM), F32)],
        compiler_params=_params(3),
        name="attention",
    )(page_table, bias, q, k, v, sbias_rows, sq, cache_k, cache_v)


def _col_tiles(offset, tn):
    assert offset % tn == 0
    return offset // tn


TM_WIDE = 1024
TM_NARROW = 512
TN = 512
TM_FULL_ROWS = 512
TN_CONV = 256


def _layer(xp, xs, mods_p, mods_s, lw, cw, bias_smem, bias_rows, ck, cv, page_table, p0, p1, db):
    (g_mix, g_ffn, w_in, w_attn_out, w_conv_out, w_out, w_ffn_in, w_ffn_out) = lw
    shift1_p, scale1_p, gate1_p, shift2_p, scale2_p, gate2_p = mods_p
    shift1_s, scale1_s, gate1_s, shift2_s, scale2_s, gate2_s = mods_s
    bp, sp, d = xp.shape
    m = bp * sp
    n_heads = bias_smem.shape[1]
    d_attn = w_attn_out.shape[0]
    d_conv = w_conv_out.shape[0]
    d_ff = w_ffn_out.shape[0]
    off_k, off_v = d_attn, 2 * d_attn
    off_cb = 3 * d_attn
    off_cc, off_ch = off_cb + d_conv, off_cb + 2 * d_conv
    off_ga = off_cb + 3 * d_conv
    off_gb = off_ga + d

    def fused(body, name, *, tm, tn, cols, acts, side_acts, weights, outs, rows=(), side_rows=(),
              tiles=(), side_tiles=(), side_body=None, side_outs=None, extra_outs=(),
              extra_scratch=()):
        res = _ws_call(
            body, name=name, m=m, tm=tm, tn=tn, n_tiles=cols // tn, tiles_per_batch=sp // tm,
            acts=acts, weights=weights, rows=rows, tiles=tiles, outs=outs,
            extra_outs=extra_outs, extra_scratch=extra_scratch,
            side=dict(body=side_body or body, acts=side_acts, rows=side_rows, tiles=side_tiles,
                      outs=outs if side_outs is None else side_outs))
        n_main = len(outs) + len(extra_outs)
        return res[:n_main], res[n_main:]

    def pad_rows(a):
        return jnp.pad(a, ((0, SAMPLE_ROWS - a.shape[0]), (0, 0)))

    h_p = _prenorm(xp, g_mix, scale1_p, shift1_p).reshape(m, d)
    h_s = _prenorm(xs, g_mix, scale1_s, shift1_s).reshape(SAMPLE_ROWS, d)
    (q_p, k_p, v_p, kb_p, vb_p), (q_s, k_s, v_s, _, _) = fused(
        _qkv_body, "qkv", tm=TM_NARROW, tn=TN, cols=d_attn, acts=[h_p], side_acts=[h_s],
        weights=[(w_in, 0), (w_in, _col_tiles(off_k, TN)), (w_in, _col_tiles(off_v, TN))],
        outs=[(d_attn, BF16), (d_attn, F32), (d_attn, F32), (d_attn, BF16), (d_attn, BF16)])

    shp = (bp, sp, d_attn)
    q3 = q_s[:db].astype(F32).reshape(db, n_heads, HEAD_DIM)
    o_attn_p, o_attn_s = _attention(q_p.reshape(shp), kb_p.reshape(shp), vb_p.reshape(shp),
                                    bias_smem, q3, ck, cv, page_table, bias_rows)
    o_attn_p = o_attn_p.reshape(m, d_attn)
    o_attn_s = pad_rows(o_attn_s.reshape(db, d_attn)).astype(BF16)

    tpb_conv = sp // TM_WIDE
    state_spec = pl.BlockSpec((None, CONV_W - 1, TN_CONV), lambda n, i: (i // tpb_conv, 0, n))
    cw_rows = cw.reshape(1, SUBLANES, d_conv)
    (o_conv_p, state_p), (o_conv_s, u_s) = fused(
        functools.partial(_conv_prompt_body, tpb=tpb_conv), "conv", tm=TM_WIDE, tn=TN_CONV,
        cols=d_conv, acts=[h_p], side_acts=[h_s],
        weights=[(w_in, _col_tiles(off_cb, TN_CONV)), (w_in, _col_tiles(off_cc, TN_CONV)),
                 (w_in, _col_tiles(off_ch, TN_CONV))],
        rows=[jnp.broadcast_to(cw_rows, (bp, SUBLANES, d_conv))], side_rows=[cw_rows],
        side_tiles=[p0, p1], outs=[(d_conv, BF16)],
        extra_outs=[((bp, CONV_W - 1, d_conv), F32, state_spec)],
        extra_scratch=[pltpu.VMEM((SUBLANES, TN_CONV), F32)],
        side_body=_conv_sample_body, side_outs=[(d_conv, BF16), (d_conv, F32)])
    state_s = jnp.stack([p1[:db], u_s[:db]], axis=1)

    (merged_p,), (merged_s,) = fused(
        _merge_body, "merge", tm=TM_NARROW, tn=TN, cols=d,
        acts=[h_p, o_attn_p, o_conv_p], side_acts=[h_s, o_attn_s, o_conv_s],
        weights=[(w_in, _col_tiles(off_ga, TN)), (w_in, _col_tiles(off_gb, TN)),
                 (w_attn_out, 0), (w_conv_out, 0)],
        outs=[(d, BF16)])
    g_rows = g_ffn.reshape(1, 1, d)
    (x1_p, h2_p), (x1_s, h2_s) = fused(
        _residual_prenorm_body, "mix_out", tm=TM_FULL_ROWS, tn=d, cols=d, acts=[merged_p],
        side_acts=[merged_s], weights=[(w_out, 0)],
        rows=[gate1_p, jnp.broadcast_to(g_rows, (bp, 1, d)), scale2_p, shift2_p],
        side_rows=[gate1_s, g_rows, scale2_s, shift2_s],
        tiles=[xp.reshape(m, d)], side_tiles=[xs.reshape(SAMPLE_ROWS, d)],
        outs=[(d, F32), (d, BF16)])
    (act_p,), (act_s,) = fused(
        _swiglu_body, "ffn_in", tm=TM_WIDE, tn=TN, cols=d_ff, acts=[h2_p],
        side_acts=[h2_s],
        weights=[(w_ffn_in, 0), (w_ffn_in, _col_tiles(d_ff, TN))], outs=[(d_ff, BF16)])
    (x2_p,), (x2_s,) = fused(
        _residual_body, "ffn_out", tm=TM_NARROW, tn=TN, cols=d, acts=[act_p], side_acts=[act_s],
        weights=[(w_ffn_out, 0)], rows=[gate2_p], side_rows=[gate2_s],
        tiles=[x1_p], side_tiles=[x1_s], outs=[(d, F32)])
    return (x2_p.reshape(bp, sp, d), x2_s.reshape(1, SAMPLE_ROWS, d),
            k_p, v_p, state_p, k_s, v_s, state_s)


def kernel(x_prompt, x_sample, cache_k, cache_v, state_conv, page_table, c_prompt, c_sample,
           g_mix, g_ffn, g_final, w_mod, b_mod, w_in, b_sb, conv_w, w_attn_out, w_conv_out,
           w_out, w_ffn_in, w_ffn_out):
    depth = w_in.shape[0]
    bp, sp, d = x_prompt.shape
    db, ds, _ = x_sample.shape
    assert ds == 1 and db <= SAMPLE_ROWS
    n_heads = b_sb.shape[1]
    page = cache_k.shape[2]
    pad = SAMPLE_ROWS - db

    def pad_rows(a):
        return jnp.pad(a, ((0, pad), (0, 0)))

    xp = x_prompt
    xs = pad_rows(x_sample.reshape(db, d)).reshape(1, SAMPLE_ROWS, d)
    c_all = jnp.concatenate([c_prompt, c_sample], axis=0)
    c_all = jnp.pad(c_all, ((0, -c_all.shape[0] % SAMPLE_ROWS), (0, 0)))
    g_final2 = g_final.reshape(1, d)
    outs = {k: [] for k in ("kp", "vp", "cp", "ks", "vs", "cs")}

    for l in range(depth):
        mod, = _ws_call(
            _mod_body, name="mod", m=c_all.shape[0], tm=c_all.shape[0], tn=1024,
            n_tiles=N_MOD * d // 1024, tiles_per_batch=1, acts=[c_all],
            weights=[(w_mod[l], 0)], rows=[b_mod[l].reshape(1, 1, N_MOD * d)],
            outs=[(N_MOD * d, F32)])
        mods_p = [mod[:bp, i * d:(i + 1) * d].reshape(bp, 1, d) for i in range(N_MOD)]
        mods_s = [pad_rows(mod[bp:bp + db, i * d:(i + 1) * d]).reshape(1, SAMPLE_ROWS, d)
                  for i in range(N_MOD)]
        lw = (g_mix[l].reshape(1, d), g_ffn[l].reshape(1, d), w_in[l], w_attn_out[l],
              w_conv_out[l], w_out[l], w_ffn_in[l], w_ffn_out[l])
        cw = jnp.pad(conv_w[l], ((0, SUBLANES - CONV_W), (0, 0)))
        bias_smem = b_sb[l].reshape(1, n_heads)
        bias_rows = jnp.broadcast_to(b_sb[l].reshape(n_heads, 1), (n_heads, page))
        p0 = pad_rows(state_conv[l][:, 0, :])
        p1 = pad_rows(state_conv[l][:, 1, :])
        xp, xs, kp, vp, csp, ks, vs, css = _layer(
            xp, xs, mods_p, mods_s, lw, cw, bias_smem, bias_rows, cache_k[l], cache_v[l],
            page_table, p0, p1, db)

        outs["kp"].append(kp.reshape(bp, sp, n_heads, HEAD_DIM))
        outs["vp"].append(vp.reshape(bp, sp, n_heads, HEAD_DIM))
        outs["cp"].append(csp)
        outs["ks"].append(ks[:db].reshape(db, 1, n_heads, HEAD_DIM))
        outs["vs"].append(vs[:db].reshape(db, 1, n_heads, HEAD_DIM))
        outs["cs"].append(css)

    y_prompt = _finalnorm(xp, g_final2)
    y_sample = _finalnorm(xs, g_final2)[0, :db].reshape(db, 1, d)
    return (y_prompt, y_sample, jnp.stack(outs["kp"]), jnp.stack(outs["vp"]),
            jnp.stack(outs["cp"]), jnp.stack(outs["ks"]), jnp.stack(outs["vs"]),
            jnp.stack(outs["cs"]))
```

```python
import functools

import jax
import jax.numpy as jnp
from jax import lax
from jax.experimental import pallas as pl
from jax.experimental.pallas import tpu as pltpu

F32 = jnp.float32
BF16 = jnp.bfloat16

EPS = 1e-6
N_MOD = 6
CONV_W = 3
HEAD_DIM = 128
LANES = 128
SUBLANES = 8
SAMPLE_ROWS = 16
VMEM_LIMIT = 56 * 1024 * 1024
CAST_ROWS = 256


def _params(n_axes):
    return pltpu.CompilerParams(
        dimension_semantics=("arbitrary",) * n_axes,
        vmem_limit_bytes=VMEM_LIMIT)


LOG2E = 1.4426950408889634


def _softplus2(z2):
    return jnp.maximum(z2, 0.0) + jnp.log2(1.0 + jnp.exp2(-jnp.abs(z2)))


def _sigmoid(z):
    return 1.0 / (1.0 + jnp.exp(-z))


def _neg_tri(n):
    r = lax.broadcasted_iota(jnp.int32, (n, n), 0)
    c = lax.broadcasted_iota(jnp.int32, (n, n), 1)
    return jnp.where(r > c, -1.0, 0.0).astype(BF16)


def _rms(x):
    return x * lax.rsqrt(jnp.mean(x * x, axis=-1, keepdims=True) + EPS)


def _prenorm_kernel(x_ref, g_ref, scale_ref, shift_ref, h_ref):
    h = _rms(x_ref[...]) * g_ref[...]
    h_ref[...] = (h * (1.0 + scale_ref[...]) + shift_ref[...]).astype(h_ref.dtype)


def _finalnorm_kernel(x_ref, g_ref, y_ref):
    y_ref[...] = _rms(x_ref[...]) * g_ref[...]


def _row_tile(s):
    return min(s, 1024)


def _prenorm(x, g, scale, shift):
    b, s, d = x.shape
    tm = _row_tile(s)
    r = scale.shape[1]
    mod_spec = pl.BlockSpec((None, r, d), lambda i, j: (i, 0, 0))
    return pl.pallas_call(
        _prenorm_kernel,
        grid=(b, s // tm),
        in_specs=[pl.BlockSpec((None, tm, d), lambda i, j: (i, j, 0)),
                  pl.BlockSpec((1, d), lambda i, j: (0, 0)),
                  mod_spec, mod_spec],
        out_specs=pl.BlockSpec((None, tm, d), lambda i, j: (i, j, 0)),
        out_shape=jax.ShapeDtypeStruct((b, s, d), BF16),
        compiler_params=_params(2),
        name="prenorm",
    )(x, g, scale, shift)


def _finalnorm(x, g):
    b, s, d = x.shape
    tm = _row_tile(s)
    return pl.pallas_call(
        _finalnorm_kernel,
        grid=(b, s // tm),
        in_specs=[pl.BlockSpec((None, tm, d), lambda i, j: (i, j, 0)),
                  pl.BlockSpec((1, d), lambda i, j: (0, 0))],
        out_specs=pl.BlockSpec((None, tm, d), lambda i, j: (i, j, 0)),
        out_shape=jax.ShapeDtypeStruct((b, s, d), F32),
        compiler_params=_params(2),
        name="finalnorm",
    )(x, g)


def _cast_weight(wb_ref, w_ref):
    def body(i, carry):
        r = pl.multiple_of(i * CAST_ROWS, CAST_ROWS)
        wb_ref[pl.ds(r, CAST_ROWS), :] = w_ref[pl.ds(r, CAST_ROWS), :].astype(BF16)
        return carry
    lax.fori_loop(0, w_ref.shape[0] // CAST_ROWS, body, 0)


def _ws_kernel(*refs, body, side_body, counts):
    refs = iter(refs)
    acts, ws, others, side_acts, side_others, outs, side_outs = (
        [next(refs) for _ in range(n)] for n in counts)
    scratch = list(refs)
    wbs, rest = scratch[:len(ws)], scratch[len(ws):]

    @pl.when(pl.program_id(1) == 0)
    def _():
        for wb, w in zip(wbs, ws):
            _cast_weight(wb, w)
        if side_body is not None:
            side_body(side_acts, wbs, side_others, side_outs, rest)

    body(acts, wbs, others, outs, rest)


def _ws_call(body, *, name, m, tm, tn, n_tiles, tiles_per_batch, acts, weights,
             rows=(), tiles=(), outs=(), extra_outs=(), extra_scratch=(), side=None):
    tpb = tiles_per_batch
    in_specs, args = [], []
    for a in acts:
        in_specs.append(pl.BlockSpec((tm, a.shape[1]), lambda n, i: (i, 0)))
        args.append(a)
    for w, off in weights:
        in_specs.append(pl.BlockSpec((w.shape[0], tn), lambda n, i, off=off: (0, off + n)))
        args.append(w)
    for r in rows:
        in_specs.append(pl.BlockSpec((None, r.shape[1], tn), lambda n, i: (i // tpb, 0, n)))
        args.append(r)
    for t in tiles:
        in_specs.append(pl.BlockSpec((tm, tn), lambda n, i: (i, n)))
        args.append(t)
    n_main_in = len(args)
    out_specs = [pl.BlockSpec((tm, tn), lambda n, i: (i, n)) for _ in outs]
    out_shape = [jax.ShapeDtypeStruct((m, cols), dt) for cols, dt in outs]
    for shape, dt, spec in extra_outs:
        out_specs.append(spec)
        out_shape.append(jax.ShapeDtypeStruct(shape, dt))
    n_main_out = len(out_shape)
    side = side or dict(body=None, acts=(), rows=(), tiles=(), outs=())
    for a in side["acts"]:
        in_specs.append(pl.BlockSpec(a.shape, lambda n, i: (0, 0)))
        args.append(a)
    for r in side["rows"]:
        in_specs.append(pl.BlockSpec((None, r.shape[1], tn), lambda n, i: (0, 0, n)))
        args.append(r)
    for t in side["tiles"]:
        in_specs.append(pl.BlockSpec((t.shape[0], tn), lambda n, i: (0, n)))
        args.append(t)
    for cols, dt in side["outs"]:
        side_rows = side["acts"][0].shape[0]
        out_specs.append(pl.BlockSpec((side_rows, tn), lambda n, i: (0, n)))
        out_shape.append(jax.ShapeDtypeStruct((side_rows, cols), dt))
    counts = (len(acts), len(weights), n_main_in - len(acts) - len(weights),
              len(side["acts"]), len(side["rows"]) + len(side["tiles"]),
              n_main_out, len(side["outs"]))
    scratch = [pltpu.VMEM((w.shape[0], tn), BF16) for w, _ in weights] + list(extra_scratch)
    kern = functools.partial(_ws_kernel, body=body, side_body=side["body"], counts=counts)
    return pl.pallas_call(
        kern,
        grid=(n_tiles, m // tm),
        in_specs=in_specs,
        out_specs=out_specs,
        out_shape=out_shape,
        scratch_shapes=scratch,
        compiler_params=_params(2),
        name=name,
    )(*args)


def _dot(a, wb_ref):
    return jnp.dot(a, wb_ref[...], preferred_element_type=F32)


def _mod_body(acts, wbs, others, outs, scratch):
    c = acts[0][...]
    a = (c * _sigmoid(c)).astype(BF16)
    outs[0][...] = _dot(a, wbs[0]) + others[0][...]


def _qkv_body(acts, wbs, others, outs, scratch):
    a = acts[0][...]
    outs[0][...] = _dot(a, wbs[0]).astype(BF16)
    k = _dot(a, wbs[1])
    v = _dot(a, wbs[2])
    outs[1][...] = k
    outs[2][...] = v
    outs[3][...] = k.astype(BF16)
    outs[4][...] = v.astype(BF16)


def _conv_prompt_body(acts, wbs, others, outs, scratch, *, tpb):
    cw_ref, = others
    o_ref, state_ref = outs
    tail_ref, = scratch
    a = acts[0][...]
    tm = a.shape[0]
    pos = pl.program_id(1) % tpb

    @pl.when(pos == 0)
    def _():
        tail_ref[...] = jnp.zeros(tail_ref.shape, F32)

    u = _dot(a, wbs[1]) * _dot(a, wbs[2])
    prev1 = tail_ref[SUBLANES - 1:SUBLANES, :]
    prev2 = tail_ref[SUBLANES - 2:SUBLANES - 1, :]
    row = lax.broadcasted_iota(jnp.int32, u.shape, 0)
    u1 = jnp.where(row == 0, prev1, pltpu.roll(u, 1, 0))
    u2 = jnp.where(row == 0, prev2, jnp.where(row == 1, prev1, pltpu.roll(u, 2, 0)))
    cw = cw_ref[...]
    conv = cw[0:1, :] * u2 + cw[1:2, :] * u1 + cw[2:3, :] * u
    o_ref[...] = (_dot(a, wbs[0]) * conv).astype(o_ref.dtype)
    tail_ref[...] = u[tm - SUBLANES:, :]

    @pl.when(pos == tpb - 1)
    def _():
        state_ref[...] = u[tm - (CONV_W - 1):, :]


def _conv_sample_body(acts, wbs, others, outs, scratch):
    cw_ref, = others[:1]
    p0_ref, p1_ref = others[1:]
    o_ref, u_ref = outs
    a = acts[0][...]
    u = _dot(a, wbs[1]) * _dot(a, wbs[2])
    cw = cw_ref[...]
    conv = cw[0:1, :] * p0_ref[...] + cw[1:2, :] * p1_ref[...] + cw[2:3, :] * u
    o_ref[...] = (_dot(a, wbs[0]) * conv).astype(o_ref.dtype)
    u_ref[...] = u


def _merge_body(acts, wbs, others, outs, scratch):
    h = acts[0][...]
    o_attn = acts[1][...]
    o_conv = acts[2][...]
    ga = _sigmoid(_dot(h, wbs[0]))
    gb = _sigmoid(_dot(h, wbs[1]))
    merged = ga * _dot(o_attn, wbs[2]) + gb * _dot(o_conv, wbs[3])
    outs[0][...] = merged.astype(outs[0].dtype)


def _residual_body(acts, wbs, others, outs, scratch):
    gate_ref, x_ref = others
    outs[0][...] = x_ref[...] + gate_ref[...] * _dot(acts[0][...], wbs[0])


def _residual_prenorm_body(acts, wbs, others, outs, scratch):
    gate_ref, g_ref, scale_ref, shift_ref, x_ref = others
    x1 = x_ref[...] + gate_ref[...] * _dot(acts[0][...], wbs[0])
    outs[0][...] = x1
    h = _rms(x1) * g_ref[...]
    outs[1][...] = (h * (1.0 + scale_ref[...]) + shift_ref[...]).astype(outs[1].dtype)


def _swiglu_body(acts, wbs, others, outs, scratch):
    a = acts[0][...]
    up = _dot(a, wbs[0])
    outs[0][...] = (up * _sigmoid(up) * _dot(a, wbs[1])).astype(outs[0].dtype)


ATTN_BLOCK = 256
ATTN_HEADS = 4
PAGE_UNIT = 2
PAGE_SLOTS = 4


def _attn_kernel(pt_ref, bias_ref, q_ref, k_ref, v_ref, sbias_ref, sq_ref, ck_ref, cv_ref,
                 o_ref, so_ref,
                 tri_ref, acc_ref, run_ref,
                 kbuf, vbuf, sem, stri_ref, sacc_ref, srun_ref, *, tq, hps):
    lin = pl.program_id(0) * pl.num_programs(1) + pl.program_id(1)
    group = pl.program_id(1)
    qi = pl.program_id(2)
    n_q = pl.num_programs(2)
    row = lax.broadcasted_iota(jnp.int32, (tq, tq), 0)
    col = lax.broadcasted_iota(jnp.int32, (tq, tq), 1)
    causal = col < row
    scale2 = HEAD_DIM ** -0.5 * LOG2E

    n_seq, n_pages = pt_ref.shape
    unit, n_heads, page, _ = kbuf.shape[1:]
    units_per_seq = n_pages // unit
    n_units = n_seq * units_per_seq
    unit_rows = unit * n_heads * page
    head_row = lax.broadcasted_iota(jnp.int32, (n_heads, page), 0)

    def unit_copies(u):
        first_page = n_pages - (u % units_per_seq + 1) * unit
        slot = u % PAGE_SLOTS
        copies = []
        for i in range(unit):
            page_id = pt_ref[u // units_per_seq, first_page + i]
            for h in range(n_heads):
                copies.append(pltpu.make_async_copy(
                    ck_ref.at[page_id, :, h, :], kbuf.at[slot, i, h], sem.at[slot]))
                copies.append(pltpu.make_async_copy(
                    cv_ref.at[page_id, :, h, :], vbuf.at[slot, i, h], sem.at[slot]))
        return copies

    def start(u):
        for c in unit_copies(u):
            c.start()

    def wait(u):
        for c in unit_copies(u):
            c.wait()

    @pl.when((lin == 0) & (qi == 0))
    def _():
        for u in range(PAGE_SLOTS - 1):
            start(u)
        stri_ref[...] = _neg_tri(page)
        sacc_ref[...] = jnp.zeros(sacc_ref.shape, F32)
        srun_ref[...] = jnp.zeros(srun_ref.shape, F32)

    @pl.when(qi == 0)
    def _():
        tri_ref[...] = _neg_tri(tq)

    def maybe(cond, guarded, fn):
        if guarded:
            pl.when(cond)(fn)
        else:
            fn()

    heads = range(hps)
    lanes = [slice(i * HEAD_DIM, (i + 1) * HEAD_DIM) for i in heads]
    bias2 = [bias_ref[0, group * hps + i] * LOG2E for i in heads]
    sbias2 = jnp.concatenate([sbias_ref[...]] * unit, axis=0) * LOG2E

    def iteration(block, t, diagonal, guarded):
        start_row = pl.multiple_of(block * tq, tq)
        live = t < n_units
        slot = t % PAGE_SLOTS
        seq = jnp.minimum(t // units_per_seq, n_seq - 1)
        maybe(live, guarded, lambda: wait(t))

        z2 = [lax.dot_general(q_ref[:, lanes[i]], k_ref[pl.ds(start_row, tq), lanes[i]],
                              (((1,), (1,)), ((), ())), preferred_element_type=F32)
              * scale2 + bias2[i] for i in heads]
        kall = kbuf[slot].reshape(unit_rows, HEAD_DIM).astype(BF16)
        cross = lax.dot_general(sq_ref[seq].astype(BF16), kall, (((1,), (1,)), ((), ())),
                                preferred_element_type=F32)

        log_beta, drop, split = [], [], []
        for i in heads:
            sp2 = _softplus2(z2[i])
            log_beta.append(z2[i] - sp2)
            drop.append(jnp.where(causal, sp2, 0.0) if diagonal else sp2)
            split.append(drop[i].astype(BF16))
        pieces = []
        for i in range(unit):
            s = jnp.zeros((n_heads, page), F32)
            for h in range(n_heads):
                c0 = (i * n_heads + h) * page
                s = s + jnp.where(head_row == h, cross[:, c0:c0 + page], 0.0)
            pieces.append(s)
        sz2 = jnp.concatenate(pieces, axis=0) * scale2 + sbias2
        ssp2 = _softplus2(sz2)
        ssplit = ssp2.astype(BF16)

        suffix = [jnp.dot(s, tri_ref[...], preferred_element_type=F32) for s in split]
        ssuffix = jnp.dot(ssplit, stri_ref[...], preferred_element_type=F32)

        if diagonal:
            w = [jnp.where(causal, jnp.exp2(log_beta[i] + suffix[i]), 0.0) for i in heads]
        else:
            w = [jnp.exp2(log_beta[i] + suffix[i] + run_ref[i]) for i in heads]
        page_sum = jnp.sum(ssp2, axis=1, keepdims=True)
        fresh = live & (t % units_per_seq == 0)
        tail = jnp.where(fresh, 0.0, srun_ref[...])
        later = [None] * unit
        for i in reversed(range(unit)):
            later[i] = tail
            tail = tail - page_sum[i * n_heads:(i + 1) * n_heads]
        srun_ref[...] = tail
        sw = jnp.exp2((sz2 - ssp2) + ssuffix + jnp.concatenate(later, axis=0))
        sw = jnp.where(live, sw, 0.0)
        sw_blocks = jnp.concatenate(
            [jnp.where(head_row == h, sw[i * n_heads:(i + 1) * n_heads], 0.0)
             for i in range(unit) for h in range(n_heads)], axis=1).astype(BF16)

        pv = [jnp.dot(w[i].astype(BF16), v_ref[pl.ds(start_row, tq), lanes[i]],
                      preferred_element_type=F32) for i in heads]
        vall = vbuf[slot].reshape(unit_rows, HEAD_DIM).astype(BF16)
        sacc = (jnp.where(fresh, 0.0, sacc_ref[...])
                + jnp.dot(sw_blocks, vall, preferred_element_type=F32))
        sacc_ref[...] = sacc
        so_ref[seq] = sacc
        for i in heads:
            mass = jnp.sum(drop[i], axis=1, keepdims=True)
            if diagonal:
                acc_ref[i] = pv[i]
                run_ref[i] = -mass
            else:
                acc_ref[i] += pv[i]
                run_ref[i] -= mass
        ahead = t + PAGE_SLOTS - 1
        maybe(ahead < n_units, guarded, lambda: start(ahead))

    t0 = lin * (n_q * (n_q + 1) // 2) + qi * (qi + 1) // 2

    def step(guarded):
        iteration(qi, t0, True, guarded)

        def body(j, carry):
            iteration(qi - 1 - j, t0 + 1 + j, False, guarded)
            return carry

        lax.fori_loop(0, qi, body, 0)
        for i in heads:
            o_ref[:, lanes[i]] = acc_ref[i].astype(o_ref.dtype)

    unguarded = t0 + qi + PAGE_SLOTS - 1 < n_units
    pl.when(unguarded)(lambda: step(False))
    pl.when(jnp.logical_not(unguarded))(lambda: step(True))


def _attention(q, k, v, bias, sq, cache_k, cache_v, page_table, sbias_rows):
    b, s, d = q.shape
    tq, hps = ATTN_BLOCK, ATTN_HEADS
    n_heads = d // HEAD_DIM
    width = hps * HEAD_DIM
    db = sq.shape[0]
    page = cache_k.shape[1]
    n_pages = page_table.shape[1]
    n_q = s // tq
    iterations = b * (n_heads // hps) * (n_q * (n_q + 1) // 2)
    assert n_pages % PAGE_UNIT == 0 and s % tq == 0 and n_heads % hps == 0
    n_units = db * (n_pages // PAGE_UNIT)
    assert PAGE_SLOTS - 1 <= n_units <= iterations
    kv_spec = pl.BlockSpec((None, s, width), lambda i, g, j, pt: (i, 0, g))
    q_spec = pl.BlockSpec((None, tq, width), lambda i, g, j, pt: (i, j, g))
    whole = lambda shape: pl.BlockSpec(shape, lambda i, g, j, pt: (0,) * len(shape))
    slot_shape = (PAGE_SLOTS, PAGE_UNIT, n_heads, page, HEAD_DIM)
    grid_spec = pltpu.PrefetchScalarGridSpec(
        num_scalar_prefetch=1,
        grid=(b, n_heads // hps, n_q),
        in_specs=[pl.BlockSpec(memory_space=pltpu.SMEM), q_spec, kv_spec, kv_spec,
                  whole((n_heads, page)), whole((db, n_heads, HEAD_DIM)),
                  pl.BlockSpec(memory_space=pl.ANY), pl.BlockSpec(memory_space=pl.ANY)],
        out_specs=[q_spec, whole((db, n_heads, HEAD_DIM))],
        scratch_shapes=[pltpu.VMEM((tq, tq), BF16),
                        pltpu.VMEM((hps, tq, HEAD_DIM), F32),
                        pltpu.VMEM((hps, tq, 1), F32),
                        pltpu.VMEM(slot_shape, F32),
                        pltpu.VMEM(slot_shape, F32),
                        pltpu.SemaphoreType.DMA((PAGE_SLOTS,)),
                        pltpu.VMEM((page, page), BF16),
                        pltpu.VMEM((n_heads, HEAD_DIM), F32),
                        pltpu.VMEM((n_heads, 1), F32)])
    return pl.pallas_call(
        functools.partial(_attn_kernel, tq=tq, hps=hps),
        grid_spec=grid_spec,
        out_shape=[jax.ShapeDtypeStruct((b, s, d), BF16),
                   jax.ShapeDtypeStruct((db, n_heads, HEAD_DIM), F32)],
        compiler_params=_params(3),
        name="attention",
    )(page_table, bias, q, k, v, sbias_rows, sq, cache_k, cache_v)


def _col_tiles(offset, tn):
    assert offset % tn == 0
    return offset // tn


TM_WIDE = 1024
TM_NARROW = 512
TN = 512
TM_FULL_ROWS = 512


def _layer(xp, xs, mods_p, mods_s, lw, cw, bias_smem, bias_rows, ck, cv, page_table, p0, p1, db):
    (g_mix, g_ffn, w_in, w_attn_out, w_conv_out, w_out, w_ffn_in, w_ffn_out) = lw
    shift1_p, scale1_p, gate1_p, shift2_p, scale2_p, gate2_p = mods_p
    shift1_s, scale1_s, gate1_s, shift2_s, scale2_s, gate2_s = mods_s
    bp, sp, d = xp.shape
    m = bp * sp
    n_heads = bias_smem.shape[1]
    d_attn = w_attn_out.shape[0]
    d_conv = w_conv_out.shape[0]
    d_ff = w_ffn_out.shape[0]
    off_k, off_v = d_attn, 2 * d_attn
    off_cb = 3 * d_attn
    off_cc, off_ch = off_cb + d_conv, off_cb + 2 * d_conv
    off_ga = off_cb + 3 * d_conv
    off_gb = off_ga + d

    def fused(body, name, *, tm, tn, cols, acts, side_acts, weights, outs, rows=(), side_rows=(),
              tiles=(), side_tiles=(), side_body=None, side_outs=None, extra_outs=(),
              extra_scratch=()):
        res = _ws_call(
            body, name=name, m=m, tm=tm, tn=tn, n_tiles=cols // tn, tiles_per_batch=sp // tm,
            acts=acts, weights=weights, rows=rows, tiles=tiles, outs=outs,
            extra_outs=extra_outs, extra_scratch=extra_scratch,
            side=dict(body=side_body or body, acts=side_acts, rows=side_rows, tiles=side_tiles,
                      outs=outs if side_outs is None else side_outs))
        n_main = len(outs) + len(extra_outs)
        return res[:n_main], res[n_main:]

    def pad_rows(a):
        return jnp.pad(a, ((0, SAMPLE_ROWS - a.shape[0]), (0, 0)))

    h_p = _prenorm(xp, g_mix, scale1_p, shift1_p).reshape(m, d)
    h_s = _prenorm(xs, g_mix, scale1_s, shift1_s).reshape(SAMPLE_ROWS, d)
    (q_p, k_p, v_p, kb_p, vb_p), (q_s, k_s, v_s, _, _) = fused(
        _qkv_body, "qkv", tm=TM_NARROW, tn=TN, cols=d_attn, acts=[h_p], side_acts=[h_s],
        weights=[(w_in, 0), (w_in, _col_tiles(off_k, TN)), (w_in, _col_tiles(off_v, TN))],
        outs=[(d_attn, BF16), (d_attn, F32), (d_attn, F32), (d_attn, BF16), (d_attn, BF16)])

    shp = (bp, sp, d_attn)
    q3 = q_s[:db].astype(F32).reshape(db, n_heads, HEAD_DIM)
    o_attn_p, o_attn_s = _attention(q_p.reshape(shp), kb_p.reshape(shp), vb_p.reshape(shp),
                                    bias_smem, q3, ck, cv, page_table, bias_rows)
    o_attn_p = o_attn_p.reshape(m, d_attn)
    o_attn_s = pad_rows(o_attn_s.reshape(db, d_attn)).astype(BF16)

    tpb_conv = sp // TM_NARROW
    state_spec = pl.BlockSpec((None, CONV_W - 1, TN), lambda n, i: (i // tpb_conv, 0, n))
    cw_rows = cw.reshape(1, SUBLANES, d_conv)
    (o_conv_p, state_p), (o_conv_s, u_s) = fused(
        functools.partial(_conv_prompt_body, tpb=tpb_conv), "conv", tm=TM_NARROW, tn=TN,
        cols=d_conv, acts=[h_p], side_acts=[h_s],
        weights=[(w_in, _col_tiles(off_cb, TN)), (w_in, _col_tiles(off_cc, TN)),
                 (w_in, _col_tiles(off_ch, TN))],
        rows=[jnp.broadcast_to(cw_rows, (bp, SUBLANES, d_conv))], side_rows=[cw_rows],
        side_tiles=[p0, p1], outs=[(d_conv, BF16)],
        extra_outs=[((bp, CONV_W - 1, d_conv), F32, state_spec)],
        extra_scratch=[pltpu.VMEM((SUBLANES, TN), F32)],
        side_body=_conv_sample_body, side_outs=[(d_conv, BF16), (d_conv, F32)])
    state_s = jnp.stack([p1[:db], u_s[:db]], axis=1)

    (merged_p,), (merged_s,) = fused(
        _merge_body, "merge", tm=TM_NARROW, tn=TN, cols=d,
        acts=[h_p, o_attn_p, o_conv_p], side_acts=[h_s, o_attn_s, o_conv_s],
        weights=[(w_in, _col_tiles(off_ga, TN)), (w_in, _col_tiles(off_gb, TN)),
                 (w_attn_out, 0), (w_conv_out, 0)],
        outs=[(d, BF16)])
    g_rows = g_ffn.reshape(1, 1, d)
    (x1_p, h2_p), (x1_s, h2_s) = fused(
        _residual_prenorm_body, "mix_out", tm=TM_FULL_ROWS, tn=d, cols=d, acts=[merged_p],
        side_acts=[merged_s], weights=[(w_out, 0)],
        rows=[gate1_p, jnp.broadcast_to(g_rows, (bp, 1, d)), scale2_p, shift2_p],
        side_rows=[gate1_s, g_rows, scale2_s, shift2_s],
        tiles=[xp.reshape(m, d)], side_tiles=[xs.reshape(SAMPLE_ROWS, d)],
        outs=[(d, F32), (d, BF16)])
    (act_p,), (act_s,) = fused(
        _swiglu_body, "ffn_in", tm=TM_WIDE, tn=TN, cols=d_ff, acts=[h2_p],
        side_acts=[h2_s],
        weights=[(w_ffn_in, 0), (w_ffn_in, _col_tiles(d_ff, TN))], outs=[(d_ff, BF16)])
    (x2_p,), (x2_s,) = fused(
        _residual_body, "ffn_out", tm=TM_NARROW, tn=TN, cols=d, acts=[act_p], side_acts=[act_s],
        weights=[(w_ffn_out, 0)], rows=[gate2_p], side_rows=[gate2_s],
        tiles=[x1_p], side_tiles=[x1_s], outs=[(d, F32)])
    return (x2_p.reshape(bp, sp, d), x2_s.reshape(1, SAMPLE_ROWS, d),
            k_p, v_p, state_p, k_s, v_s, state_s)


def kernel(x_prompt, x_sample, cache_k, cache_v, state_conv, page_table, c_prompt, c_sample,
           g_mix, g_ffn, g_final, w_mod, b_mod, w_in, b_sb, conv_w, w_attn_out, w_conv_out,
           w_out, w_ffn_in, w_ffn_out):
    depth = w_in.shape[0]
    bp, sp, d = x_prompt.shape
    db, ds, _ = x_sample.shape
    assert ds == 1 and db <= SAMPLE_ROWS
    n_heads = b_sb.shape[1]
    page = cache_k.shape[2]
    pad = SAMPLE_ROWS - db

    def pad_rows(a):
        return jnp.pad(a, ((0, pad), (0, 0)))

    xp = x_prompt
    xs = pad_rows(x_sample.reshape(db, d)).reshape(1, SAMPLE_ROWS, d)
    c_all = jnp.concatenate([c_prompt, c_sample], axis=0)
    c_all = jnp.pad(c_all, ((0, -c_all.shape[0] % SAMPLE_ROWS), (0, 0)))
    g_final2 = g_final.reshape(1, d)
    outs = {k: [] for k in ("kp", "vp", "cp", "ks", "vs", "cs")}

    for l in range(depth):
        mod, = _ws_call(
            _mod_body, name="mod", m=c_all.shape[0], tm=c_all.shape[0], tn=1024,
            n_tiles=N_MOD * d // 1024, tiles_per_batch=1, acts=[c_all],
            weights=[(w_mod[l], 0)], rows=[b_mod[l].reshape(1, 1, N_MOD * d)],
            outs=[(N_MOD * d, F32)])
        mods_p = [mod[:bp, i * d:(i + 1) * d].reshape(bp, 1, d) for i in range(N_MOD)]
        mods_s = [pad_rows(mod[bp:bp + db, i * d:(i + 1) * d]).reshape(1, SAMPLE_ROWS, d)
                  for i in range(N_MOD)]
        lw = (g_mix[l].reshape(1, d), g_ffn[l].reshape(1, d), w_in[l], w_attn_out[l],
              w_conv_out[l], w_out[l], w_ffn_in[l], w_ffn_out[l])
        cw = jnp.pad(conv_w[l], ((0, SUBLANES - CONV_W), (0, 0)))
        bias_smem = b_sb[l].reshape(1, n_heads)
        bias_rows = jnp.broadcast_to(b_sb[l].reshape(n_heads, 1), (n_heads, page))
        p0 = pad_rows(state_conv[l][:, 0, :])
        p1 = pad_rows(state_conv[l][:, 1, :])
        xp, xs, kp, vp, csp, ks, vs, css = _layer(
            xp, xs, mods_p, mods_s, lw, cw, bias_smem, bias_rows, cache_k[l], cache_v[l],
            page_table, p0, p1, db)

        outs["kp"].append(kp.reshape(bp, sp, n_heads, HEAD_DIM))
        outs["vp"].append(vp.reshape(bp, sp, n_heads, HEAD_DIM))
        outs["cp"].append(csp)
        outs["ks"].append(ks[:db].reshape(db, 1, n_heads, HEAD_DIM))
        outs["vs"].append(vs[:db].reshape(db, 1, n_heads, HEAD_DIM))
        outs["cs"].append(css)

    y_prompt = _finalnorm(xp, g_final2)
    y_sample = _finalnorm(xs, g_final2)[0, :db].reshape(db, 1, d)
    return (y_prompt, y_sample, jnp.stack(outs["kp"]), jnp.stack(outs["vp"]),
            jnp.stack(outs["cp"]), jnp.stack(outs["ks"]), jnp.stack(outs["vs"]),
            jnp.stack(outs["cs"]))
```

```python
import functools

import jax
import jax.numpy as jnp
from jax import lax
from jax.experimental import pallas as pl
from jax.experimental.pallas import tpu as pltpu

F32 = jnp.float32
BF16 = jnp.bfloat16

EPS = 1e-6
N_MOD = 6
CONV_W = 3
HEAD_DIM = 128
LANES = 128
SUBLANES = 8
SAMPLE_ROWS = 16
VMEM_LIMIT = 56 * 1024 * 1024
CAST_ROWS = 256


def _params(n_axes):
    return pltpu.CompilerParams(
        dimension_semantics=("arbitrary",) * n_axes,
        vmem_limit_bytes=VMEM_LIMIT)


LOG2E = 1.4426950408889634


def _softplus2(z2):
    return jnp.maximum(z2, 0.0) + jnp.log2(1.0 + jnp.exp2(-jnp.abs(z2)))


def _sigmoid(z):
    return 1.0 / (1.0 + jnp.exp(-z))


def _neg_tri(n):
    r = lax.broadcasted_iota(jnp.int32, (n, n), 0)
    c = lax.broadcasted_iota(jnp.int32, (n, n), 1)
    return jnp.where(r > c, -1.0, 0.0).astype(BF16)


def _rms(x):
    return x * lax.rsqrt(jnp.mean(x * x, axis=-1, keepdims=True) + EPS)


def _prenorm_kernel(x_ref, g_ref, scale_ref, shift_ref, h_ref):
    h = _rms(x_ref[...]) * g_ref[...]
    h_ref[...] = (h * (1.0 + scale_ref[...]) + shift_ref[...]).astype(h_ref.dtype)


def _finalnorm_kernel(x_ref, g_ref, y_ref):
    y_ref[...] = _rms(x_ref[...]) * g_ref[...]


def _row_tile(s):
    return min(s, 1024)


def _prenorm(x, g, scale, shift):
    b, s, d = x.shape
    tm = _row_tile(s)
    r = scale.shape[1]
    mod_spec = pl.BlockSpec((None, r, d), lambda i, j: (i, 0, 0))
    return pl.pallas_call(
        _prenorm_kernel,
        grid=(b, s // tm),
        in_specs=[pl.BlockSpec((None, tm, d), lambda i, j: (i, j, 0)),
                  pl.BlockSpec((1, d), lambda i, j: (0, 0)),
                  mod_spec, mod_spec],
        out_specs=pl.BlockSpec((None, tm, d), lambda i, j: (i, j, 0)),
        out_shape=jax.ShapeDtypeStruct((b, s, d), BF16),
        compiler_params=_params(2),
        name="prenorm",
    )(x, g, scale, shift)


def _finalnorm(x, g):
    b, s, d = x.shape
    tm = _row_tile(s)
    return pl.pallas_call(
        _finalnorm_kernel,
        grid=(b, s // tm),
        in_specs=[pl.BlockSpec((None, tm, d), lambda i, j: (i, j, 0)),
                  pl.BlockSpec((1, d), lambda i, j: (0, 0))],
        out_specs=pl.BlockSpec((None, tm, d), lambda i, j: (i, j, 0)),
        out_shape=jax.ShapeDtypeStruct((b, s, d), F32),
        compiler_params=_params(2),
        name="finalnorm",
    )(x, g)


def _cast_weight(wb_ref, w_ref):
    def body(i, carry):
        r = pl.multiple_of(i * CAST_ROWS, CAST_ROWS)
        wb_ref[pl.ds(r, CAST_ROWS), :] = w_ref[pl.ds(r, CAST_ROWS), :].astype(BF16)
        return carry
    lax.fori_loop(0, w_ref.shape[0] // CAST_ROWS, body, 0)


def _ws_kernel(*refs, body, side_body, counts):
    refs = iter(refs)
    acts, ws, others, side_acts, side_others, outs, side_outs = (
        [next(refs) for _ in range(n)] for n in counts)
    scratch = list(refs)
    wbs, rest = scratch[:len(ws)], scratch[len(ws):]

    @pl.when(pl.program_id(1) == 0)
    def _():
        for wb, w in zip(wbs, ws):
            _cast_weight(wb, w)
        if side_body is not None:
            side_body(side_acts, wbs, side_others, side_outs, rest)

    body(acts, wbs, others, outs, rest)


def _ws_call(body, *, name, m, tm, tn, n_tiles, tiles_per_batch, acts, weights,
             rows=(), tiles=(), outs=(), extra_outs=(), extra_scratch=(), side=None):
    tpb = tiles_per_batch
    in_specs, args = [], []
    for a in acts:
        in_specs.append(pl.BlockSpec((tm, a.shape[1]), lambda n, i: (i, 0)))
        args.append(a)
    for w, off in weights:
        in_specs.append(pl.BlockSpec((w.shape[0], tn), lambda n, i, off=off: (0, off + n)))
        args.append(w)
    for r in rows:
        in_specs.append(pl.BlockSpec((None, r.shape[1], tn), lambda n, i: (i // tpb, 0, n)))
        args.append(r)
    for t in tiles:
        in_specs.append(pl.BlockSpec((tm, tn), lambda n, i: (i, n)))
        args.append(t)
    n_main_in = len(args)
    out_specs = [pl.BlockSpec((tm, tn), lambda n, i: (i, n)) for _ in outs]
    out_shape = [jax.ShapeDtypeStruct((m, cols), dt) for cols, dt in outs]
    for shape, dt, spec in extra_outs:
        out_specs.append(spec)
        out_shape.append(jax.ShapeDtypeStruct(shape, dt))
    n_main_out = len(out_shape)
    side = side or dict(body=None, acts=(), rows=(), tiles=(), outs=())
    for a in side["acts"]:
        in_specs.append(pl.BlockSpec(a.shape, lambda n, i: (0, 0)))
        args.append(a)
    for r in side["rows"]:
        in_specs.append(pl.BlockSpec((None, r.shape[1], tn), lambda n, i: (0, 0, n)))
        args.append(r)
    for t in side["tiles"]:
        in_specs.append(pl.BlockSpec((t.shape[0], tn), lambda n, i: (0, n)))
        args.append(t)
    for cols, dt in side["outs"]:
        side_rows = side["acts"][0].shape[0]
        out_specs.append(pl.BlockSpec((side_rows, tn), lambda n, i: (0, n)))
        out_shape.append(jax.ShapeDtypeStruct((side_rows, cols), dt))
    counts = (len(acts), len(weights), n_main_in - len(acts) - len(weights),
              len(side["acts"]), len(side["rows"]) + len(side["tiles"]),
              n_main_out, len(side["outs"]))
    scratch = [pltpu.VMEM((w.shape[0], tn), BF16) for w, _ in weights] + list(extra_scratch)
    kern = functools.partial(_ws_kernel, body=body, side_body=side["body"], counts=counts)
    return pl.pallas_call(
        kern,
        grid=(n_tiles, m // tm),
        in_specs=in_specs,
        out_specs=out_specs,
        out_shape=out_shape,
        scratch_shapes=scratch,
        compiler_params=_params(2),
        name=name,
    )(*args)


def _dot(a, wb_ref):
    return jnp.dot(a, wb_ref[...], preferred_element_type=F32)


def _mod_body(acts, wbs, others, outs, scratch):
    c = acts[0][...]
    a = (c * _sigmoid(c)).astype(BF16)
    outs[0][...] = _dot(a, wbs[0]) + others[0][...]


def _qkv_body(acts, wbs, others, outs, scratch):
    a = acts[0][...]
    outs[0][...] = _dot(a, wbs[0]).astype(BF16)
    k = _dot(a, wbs[1])
    v = _dot(a, wbs[2])
    outs[1][...] = k
    outs[2][...] = v
    outs[3][...] = k.astype(BF16)
    outs[4][...] = v.astype(BF16)


def _conv_prompt_body(acts, wbs, others, outs, scratch, *, tpb):
    cw_ref, = others
    o_ref, state_ref = outs
    tail_ref, = scratch
    a = acts[0][...]
    tm = a.shape[0]
    pos = pl.program_id(1) % tpb

    @pl.when(pos == 0)
    def _():
        tail_ref[...] = jnp.zeros(tail_ref.shape, F32)

    u = _dot(a, wbs[1]) * _dot(a, wbs[2])
    prev1 = tail_ref[SUBLANES - 1:SUBLANES, :]
    prev2 = tail_ref[SUBLANES - 2:SUBLANES - 1, :]
    row = lax.broadcasted_iota(jnp.int32, u.shape, 0)
    u1 = jnp.where(row == 0, prev1, pltpu.roll(u, 1, 0))
    u2 = jnp.where(row == 0, prev2, jnp.where(row == 1, prev1, pltpu.roll(u, 2, 0)))
    cw = cw_ref[...]
    conv = cw[0:1, :] * u2 + cw[1:2, :] * u1 + cw[2:3, :] * u
    o_ref[...] = (_dot(a, wbs[0]) * conv).astype(o_ref.dtype)
    tail_ref[...] = u[tm - SUBLANES:, :]

    @pl.when(pos == tpb - 1)
    def _():
        state_ref[...] = u[tm - (CONV_W - 1):, :]


def _conv_sample_body(acts, wbs, others, outs, scratch):
    cw_ref, = others[:1]
    p0_ref, p1_ref = others[1:]
    o_ref, u_ref = outs
    a = acts[0][...]
    u = _dot(a, wbs[1]) * _dot(a, wbs[2])
    cw = cw_ref[...]
    conv = cw[0:1, :] * p0_ref[...] + cw[1:2, :] * p1_ref[...] + cw[2:3, :] * u
    o_ref[...] = (_dot(a, wbs[0]) * conv).astype(o_ref.dtype)
    u_ref[...] = u


def _merge_body(acts, wbs, others, outs, scratch):
    h = acts[0][...]
    o_attn = acts[1][...]
    o_conv = acts[2][...]
    ga = _sigmoid(_dot(h, wbs[0]))
    gb = _sigmoid(_dot(h, wbs[1]))
    merged = ga * _dot(o_attn, wbs[2]) + gb * _dot(o_conv, wbs[3])
    outs[0][...] = merged.astype(outs[0].dtype)


def _residual_body(acts, wbs, others, outs, scratch):
    gate_ref, x_ref = others
    outs[0][...] = x_ref[...] + gate_ref[...] * _dot(acts[0][...], wbs[0])


def _residual_prenorm_body(acts, wbs, others, outs, scratch):
    gate_ref, g_ref, scale_ref, shift_ref, x_ref = others
    x1 = x_ref[...] + gate_ref[...] * _dot(acts[0][...], wbs[0])
    outs[0][...] = x1
    h = _rms(x1) * g_ref[...]
    outs[1][...] = (h * (1.0 + scale_ref[...]) + shift_ref[...]).astype(outs[1].dtype)


def _swiglu_body(acts, wbs, others, outs, scratch):
    a = acts[0][...]
    up = _dot(a, wbs[0])
    outs[0][...] = (up * _sigmoid(up) * _dot(a, wbs[1])).astype(outs[0].dtype)


ATTN_BLOCK = 256
ATTN_HEADS = 4
PAGE_UNIT = 2
PAGE_SLOTS = 6


def _attn_kernel(pt_ref, bias_ref, q_ref, k_ref, v_ref, sbias_ref, sq_ref, ck_ref, cv_ref,
                 o_ref, so_ref,
                 tri_ref, acc_ref, run_ref,
                 kbuf, vbuf, sem, stri_ref, sacc_ref, srun_ref, *, tq, hps):
    lin = pl.program_id(0) * pl.num_programs(1) + pl.program_id(1)
    group = pl.program_id(1)
    qi = pl.program_id(2)
    n_q = pl.num_programs(2)
    row = lax.broadcasted_iota(jnp.int32, (tq, tq), 0)
    col = lax.broadcasted_iota(jnp.int32, (tq, tq), 1)
    causal = col < row
    scale2 = HEAD_DIM ** -0.5 * LOG2E

    n_seq, n_pages = pt_ref.shape
    unit, n_heads, page, _ = kbuf.shape[1:]
    units_per_seq = n_pages // unit
    n_units = n_seq * units_per_seq
    unit_rows = unit * n_heads * page
    head_row = lax.broadcasted_iota(jnp.int32, (n_heads, page), 0)

    def unit_copies(u):
        first_page = n_pages - (u % units_per_seq + 1) * unit
        slot = u % PAGE_SLOTS
        copies = []
        for i in range(unit):
            page_id = pt_ref[u // units_per_seq, first_page + i]
            for h in range(n_heads):
                copies.append(pltpu.make_async_copy(
                    ck_ref.at[page_id, :, h, :], kbuf.at[slot, i, h], sem.at[slot]))
                copies.append(pltpu.make_async_copy(
                    cv_ref.at[page_id, :, h, :], vbuf.at[slot, i, h], sem.at[slot]))
        return copies

    def start(u):
        for c in unit_copies(u):
            c.start()

    def wait(u):
        for c in unit_copies(u):
            c.wait()

    @pl.when((lin == 0) & (qi == 0))
    def _():
        for u in range(PAGE_SLOTS - 1):
            start(u)
        stri_ref[...] = _neg_tri(page)
        sacc_ref[...] = jnp.zeros(sacc_ref.shape, F32)
        srun_ref[...] = jnp.zeros(srun_ref.shape, F32)

    @pl.when(qi == 0)
    def _():
        tri_ref[...] = _neg_tri(tq)

    def maybe(cond, guarded, fn):
        if guarded:
            pl.when(cond)(fn)
        else:
            fn()

    heads = range(hps)
    lanes = [slice(i * HEAD_DIM, (i + 1) * HEAD_DIM) for i in heads]
    bias2 = [bias_ref[0, group * hps + i] * LOG2E for i in heads]
    sbias2 = jnp.concatenate([sbias_ref[...]] * unit, axis=0) * LOG2E

    def iteration(block, t, diagonal, guarded):
        start_row = pl.multiple_of(block * tq, tq)
        live = t < n_units
        slot = t % PAGE_SLOTS
        seq = jnp.minimum(t // units_per_seq, n_seq - 1)
        maybe(live, guarded, lambda: wait(t))

        z2 = [lax.dot_general(q_ref[:, lanes[i]], k_ref[pl.ds(start_row, tq), lanes[i]],
                              (((1,), (1,)), ((), ())), preferred_element_type=F32)
              * scale2 + bias2[i] for i in heads]
        kall = kbuf[slot].reshape(unit_rows, HEAD_DIM).astype(BF16)
        cross = lax.dot_general(sq_ref[seq].astype(BF16), kall, (((1,), (1,)), ((), ())),
                                preferred_element_type=F32)

        log_beta, drop, split = [], [], []
        for i in heads:
            sp2 = _softplus2(z2[i])
            log_beta.append(z2[i] - sp2)
            drop.append(jnp.where(causal, sp2, 0.0) if diagonal else sp2)
            split.append(drop[i].astype(BF16))
        pieces = []
        for i in range(unit):
            s = jnp.zeros((n_heads, page), F32)
            for h in range(n_heads):
                c0 = (i * n_heads + h) * page
                s = s + jnp.where(head_row == h, cross[:, c0:c0 + page], 0.0)
            pieces.append(s)
        sz2 = jnp.concatenate(pieces, axis=0) * scale2 + sbias2
        ssp2 = _softplus2(sz2)
        ssplit = ssp2.astype(BF16)

        suffix = [jnp.dot(s, tri_ref[...], preferred_element_type=F32) for s in split]
        ssuffix = jnp.dot(ssplit, stri_ref[...], preferred_element_type=F32)

        if diagonal:
            w = [jnp.where(causal, jnp.exp2(log_beta[i] + suffix[i]), 0.0) for i in heads]
        else:
            w = [jnp.exp2(log_beta[i] + suffix[i] + run_ref[i]) for i in heads]
        page_sum = jnp.sum(ssp2, axis=1, keepdims=True)
        fresh = live & (t % units_per_seq == 0)
        tail = jnp.where(fresh, 0.0, srun_ref[...])
        later = [None] * unit
        for i in reversed(range(unit)):
            later[i] = tail
            tail = tail - page_sum[i * n_heads:(i + 1) * n_heads]
        srun_ref[...] = tail
        sw = jnp.exp2((sz2 - ssp2) + ssuffix + jnp.concatenate(later, axis=0))
        sw = jnp.where(live, sw, 0.0)
        sw_blocks = jnp.concatenate(
            [jnp.where(head_row == h, sw[i * n_heads:(i + 1) * n_heads], 0.0)
             for i in range(unit) for h in range(n_heads)], axis=1).astype(BF16)

        pv = [jnp.dot(w[i].astype(BF16), v_ref[pl.ds(start_row, tq), lanes[i]],
                      preferred_element_type=F32) for i in heads]
        vall = vbuf[slot].reshape(unit_rows, HEAD_DIM).astype(BF16)
        sacc = (jnp.where(fresh, 0.0, sacc_ref[...])
                + jnp.dot(sw_blocks, vall, preferred_element_type=F32))
        sacc_ref[...] = sacc
        so_ref[seq] = sacc
        for i in heads:
            mass = jnp.sum(drop[i], axis=1, keepdims=True)
            if diagonal:
                acc_ref[i] = pv[i]
                run_ref[i] = -mass
            else:
                acc_ref[i] += pv[i]
                run_ref[i] -= mass
        ahead = t + PAGE_SLOTS - 1
        maybe(ahead < n_units, guarded, lambda: start(ahead))

    t0 = lin * (n_q * (n_q + 1) // 2) + qi * (qi + 1) // 2

    def step(guarded):
        iteration(qi, t0, True, guarded)

        def body(j, carry):
            iteration(qi - 1 - j, t0 + 1 + j, False, guarded)
            return carry

        lax.fori_loop(0, qi, body, 0)
        for i in heads:
            o_ref[:, lanes[i]] = acc_ref[i].astype(o_ref.dtype)

    unguarded = t0 + qi + PAGE_SLOTS - 1 < n_units
    pl.when(unguarded)(lambda: step(False))
    pl.when(jnp.logical_not(unguarded))(lambda: step(True))


def _attention(q, k, v, bias, sq, cache_k, cache_v, page_table, sbias_rows):
    b, s, d = q.shape
    tq, hps = ATTN_BLOCK, ATTN_HEADS
    n_heads = d // HEAD_DIM
    width = hps * HEAD_DIM
    db = sq.shape[0]
    page = cache_k.shape[1]
    n_pages = page_table.shape[1]
    n_q = s // tq
    iterations = b * (n_heads // hps) * (n_q * (n_q + 1) // 2)
    assert n_pages % PAGE_UNIT == 0 and s % tq == 0 and n_heads % hps == 0
    n_units = db * (n_pages // PAGE_UNIT)
    assert PAGE_SLOTS - 1 <= n_units <= iterations
    kv_spec = pl.BlockSpec((None, s, width), lambda i, g, j, pt: (i, 0, g))
    q_spec = pl.BlockSpec((None, tq, width), lambda i, g, j, pt: (i, j, g))
    whole = lambda shape: pl.BlockSpec(shape, lambda i, g, j, pt: (0,) * len(shape))
    slot_shape = (PAGE_SLOTS, PAGE_UNIT, n_heads, page, HEAD_DIM)
    grid_spec = pltpu.PrefetchScalarGridSpec(
        num_scalar_prefetch=1,
        grid=(b, n_heads // hps, n_q),
        in_specs=[pl.BlockSpec(memory_space=pltpu.SMEM), q_spec, kv_spec, kv_spec,
                  whole((n_heads, page)), whole((db, n_heads, HEAD_DIM)),
                  pl.BlockSpec(memory_space=pl.ANY), pl.BlockSpec(memory_space=pl.ANY)],
        out_specs=[q_spec, whole((db, n_heads, HEAD_DIM))],
        scratch_shapes=[pltpu.VMEM((tq, tq), BF16),
                        pltpu.VMEM((hps, tq, HEAD_DIM), F32),
                        pltpu.VMEM((hps, tq, 1), F32),
                        pltpu.VMEM(slot_shape, F32),
                        pltpu.VMEM(slot_shape, F32),
                        pltpu.SemaphoreType.DMA((PAGE_SLOTS,)),
                        pltpu.VMEM((page, page), BF16),
                        pltpu.VMEM((n_heads, HEAD_DIM), F32),
                        pltpu.VMEM((n_heads, 1), F32)])
    return pl.pallas_call(
        functools.partial(_attn_kernel, tq=tq, hps=hps),
        grid_spec=grid_spec,
        out_shape=[jax.ShapeDtypeStruct((b, s, d), BF16),
                   jax.ShapeDtypeStruct((db, n_heads, HEAD_DIM), F32)],
        compiler_params=_params(3),
        name="attention",
    )(page_table, bias, q, k, v, sbias_rows, sq, cache_k, cache_v)


def _col_tiles(offset, tn):
    assert offset % tn == 0
    return offset // tn


TM_WIDE = 1024
TM_NARROW = 512
TN = 512
TM_FULL_ROWS = 512


def _layer(xp, xs, mods_p, mods_s, lw, cw, bias_smem, bias_rows, ck, cv, page_table, p0, p1, db):
    (g_mix, g_ffn, w_in, w_attn_out, w_conv_out, w_out, w_ffn_in, w_ffn_out) = lw
    shift1_p, scale1_p, gate1_p, shift2_p, scale2_p, gate2_p = mods_p
    shift1_s, scale1_s, gate1_s, shift2_s, scale2_s, gate2_s = mods_s
    bp, sp, d = xp.shape
    m = bp * sp
    n_heads = bias_smem.shape[1]
    d_attn = w_attn_out.shape[0]
    d_conv = w_conv_out.shape[0]
    d_ff = w_ffn_out.shape[0]
    off_k, off_v = d_attn, 2 * d_attn
    off_cb = 3 * d_attn
    off_cc, off_ch = off_cb + d_conv, off_cb + 2 * d_conv
    off_ga = off_cb + 3 * d_conv
    off_gb = off_ga + d

    def fused(body, name, *, tm, tn, cols, acts, side_acts, weights, outs, rows=(), side_rows=(),
              tiles=(), side_tiles=(), side_body=None, side_outs=None, extra_outs=(),
              extra_scratch=()):
        res = _ws_call(
            body, name=name, m=m, tm=tm, tn=tn, n_tiles=cols // tn, tiles_per_batch=sp // tm,
            acts=acts, weights=weights, rows=rows, tiles=tiles, outs=outs,
            extra_outs=extra_outs, extra_scratch=extra_scratch,
            side=dict(body=side_body or body, acts=side_acts, rows=side_rows, tiles=side_tiles,
                      outs=outs if side_outs is None else side_outs))
        n_main = len(outs) + len(extra_outs)
        return res[:n_main], res[n_main:]

    def pad_rows(a):
        return jnp.pad(a, ((0, SAMPLE_ROWS - a.shape[0]), (0, 0)))

    h_p = _prenorm(xp, g_mix, scale1_p, shift1_p).reshape(m, d)
    h_s = _prenorm(xs, g_mix, scale1_s, shift1_s).reshape(SAMPLE_ROWS, d)
    (q_p, k_p, v_p, kb_p, vb_p), (q_s, k_s, v_s, _, _) = fused(
        _qkv_body, "qkv", tm=TM_NARROW, tn=TN, cols=d_attn, acts=[h_p], side_acts=[h_s],
        weights=[(w_in, 0), (w_in, _col_tiles(off_k, TN)), (w_in, _col_tiles(off_v, TN))],
        outs=[(d_attn, BF16), (d_attn, F32), (d_attn, F32), (d_attn, BF16), (d_attn, BF16)])

    shp = (bp, sp, d_attn)
    q3 = q_s[:db].astype(F32).reshape(db, n_heads, HEAD_DIM)
    o_attn_p, o_attn_s = _attention(q_p.reshape(shp), kb_p.reshape(shp), vb_p.reshape(shp),
                                    bias_smem, q3, ck, cv, page_table, bias_rows)
    o_attn_p = o_attn_p.reshape(m, d_attn)
    o_attn_s = pad_rows(o_attn_s.reshape(db, d_attn)).astype(BF16)

    tpb_conv = sp // TM_WIDE
    state_spec = pl.BlockSpec((None, CONV_W - 1, TN), lambda n, i: (i // tpb_conv, 0, n))
    cw_rows = cw.reshape(1, SUBLANES, d_conv)
    (o_conv_p, state_p), (o_conv_s, u_s) = fused(
        functools.partial(_conv_prompt_body, tpb=tpb_conv), "conv", tm=TM_WIDE, tn=TN,
        cols=d_conv, acts=[h_p], side_acts=[h_s],
        weights=[(w_in, _col_tiles(off_cb, TN)), (w_in, _col_tiles(off_cc, TN)),
                 (w_in, _col_tiles(off_ch, TN))],
        rows=[jnp.broadcast_to(cw_rows, (bp, SUBLANES, d_conv))], side_rows=[cw_rows],
        side_tiles=[p0, p1], outs=[(d_conv, BF16)],
        extra_outs=[((bp, CONV_W - 1, d_conv), F32, state_spec)],
        extra_scratch=[pltpu.VMEM((SUBLANES, TN), F32)],
        side_body=_conv_sample_body, side_outs=[(d_conv, BF16), (d_conv, F32)])
    state_s = jnp.stack([p1[:db], u_s[:db]], axis=1)

    (merged_p,), (merged_s,) = fused(
        _merge_body, "merge", tm=TM_NARROW, tn=TN, cols=d,
        acts=[h_p, o_attn_p, o_conv_p], side_acts=[h_s, o_attn_s, o_conv_s],
        weights=[(w_in, _col_tiles(off_ga, TN)), (w_in, _col_tiles(off_gb, TN)),
                 (w_attn_out, 0), (w_conv_out, 0)],
        outs=[(d, BF16)])
    g_rows = g_ffn.reshape(1, 1, d)
    (x1_p, h2_p), (x1_s, h2_s) = fused(
        _residual_prenorm_body, "mix_out", tm=TM_FULL_ROWS, tn=d, cols=d, acts=[merged_p],
        side_acts=[merged_s], weights=[(w_out, 0)],
        rows=[gate1_p, jnp.broadcast_to(g_rows, (bp, 1, d)), scale2_p, shift2_p],
        side_rows=[gate1_s, g_rows, scale2_s, shift2_s],
        tiles=[xp.reshape(m, d)], side_tiles=[xs.reshape(SAMPLE_ROWS, d)],
        outs=[(d, F32), (d, BF16)])
    (act_p,), (act_s,) = fused(
        _swiglu_body, "ffn_in", tm=TM_WIDE, tn=TN, cols=d_ff, acts=[h2_p],
        side_acts=[h2_s],
        weights=[(w_ffn_in, 0), (w_ffn_in, _col_tiles(d_ff, TN))], outs=[(d_ff, BF16)])
    (x2_p,), (x2_s,) = fused(
        _residual_body, "ffn_out", tm=TM_NARROW, tn=TN, cols=d, acts=[act_p], side_acts=[act_s],
        weights=[(w_ffn_out, 0)], rows=[gate2_p], side_rows=[gate2_s],
        tiles=[x1_p], side_tiles=[x1_s], outs=[(d, F32)])
    return (x2_p.reshape(bp, sp, d), x2_s.reshape(1, SAMPLE_ROWS, d),
            k_p, v_p, state_p, k_s, v_s, state_s)


def kernel(x_prompt, x_sample, cache_k, cache_v, state_conv, page_table, c_prompt, c_sample,
           g_mix, g_ffn, g_final, w_mod, b_mod, w_in, b_sb, conv_w, w_attn_out, w_conv_out,
           w_out, w_ffn_in, w_ffn_out):
    depth = w_in.shape[0]
    bp, sp, d = x_prompt.shape
    db, ds, _ = x_sample.shape
    assert ds == 1 and db <= SAMPLE_ROWS
    n_heads = b_sb.shape[1]
    page = cache_k.shape[2]
    pad = SAMPLE_ROWS - db

    def pad_rows(a):
        return jnp.pad(a, ((0, pad), (0, 0)))

    xp = x_prompt
    xs = pad_rows(x_sample.reshape(db, d)).reshape(1, SAMPLE_ROWS, d)
    c_all = jnp.concatenate([c_prompt, c_sample], axis=0)
    c_all = jnp.pad(c_all, ((0, -c_all.shape[0] % SAMPLE_ROWS), (0, 0)))
    g_final2 = g_final.reshape(1, d)
    outs = {k: [] for k in ("kp", "vp", "cp", "ks", "vs", "cs")}

    for l in range(depth):
        mod, = _ws_call(
            _mod_body, name="mod", m=c_all.shape[0], tm=c_all.shape[0], tn=1024,
            n_tiles=N_MOD * d // 1024, tiles_per_batch=1, acts=[c_all],
            weights=[(w_mod[l], 0)], rows=[b_mod[l].reshape(1, 1, N_MOD * d)],
            outs=[(N_MOD * d, F32)])
        mods_p = [mod[:bp, i * d:(i + 1) * d].reshape(bp, 1, d) for i in range(N_MOD)]
        mods_s = [pad_rows(mod[bp:bp + db, i * d:(i + 1) * d]).reshape(1, SAMPLE_ROWS, d)
                  for i in range(N_MOD)]
        lw = (g_mix[l].reshape(1, d), g_ffn[l].reshape(1, d), w_in[l], w_attn_out[l],
              w_conv_out[l], w_out[l], w_ffn_in[l], w_ffn_out[l])
        cw = jnp.pad(conv_w[l], ((0, SUBLANES - CONV_W), (0, 0)))
        bias_smem = b_sb[l].reshape(1, n_heads)
        bias_rows = jnp.broadcast_to(b_sb[l].reshape(n_heads, 1), (n_heads, page))
        p0 = pad_rows(state_conv[l][:, 0, :])
        p1 = pad_rows(state_conv[l][:, 1, :])
        xp, xs, kp, vp, csp, ks, vs, css = _layer(
            xp, xs, mods_p, mods_s, lw, cw, bias_smem, bias_rows, cache_k[l], cache_v[l],
            page_table, p0, p1, db)

        outs["kp"].append(kp.reshape(bp, sp, n_heads, HEAD_DIM))
        outs["vp"].append(vp.reshape(bp, sp, n_heads, HEAD_DIM))
        outs["cp"].append(csp)
        outs["ks"].append(ks[:db].reshape(db, 1, n_heads, HEAD_DIM))
        outs["vs"].append(vs[:db].reshape(db, 1, n_heads, HEAD_DIM))
        outs["cs"].append(css)

    y_prompt = _finalnorm(xp, g_final2)
    y_sample = _finalnorm(xs, g_final2)[0, :db].reshape(db, 1, d)
    return (y_prompt, y_sample, jnp.stack(outs["kp"]), jnp.stack(outs["vp"]),
            jnp.stack(outs["cp"]), jnp.stack(outs["ks"]), jnp.stack(outs["vs"]),
            jnp.stack(outs["cs"]))
```

```python
import functools

import jax
import jax.numpy as jnp
from jax import lax
from jax.experimental import pallas as pl
from jax.experimental.pallas import tpu as pltpu

F32 = jnp.float32
BF16 = jnp.bfloat16

EPS = 1e-6
N_MOD = 6
CONV_W = 3
HEAD_DIM = 128
SUBLANES = 8
SAMPLE_ROWS = 16
VMEM_LIMIT = 56 * 1024 * 1024
CAST_ROWS = 256


def _params(n_axes):
    return pltpu.CompilerParams(
        dimension_semantics=("arbitrary",) * n_axes,
        vmem_limit_bytes=VMEM_LIMIT)


LOG2E = 1.4426950408889634


def _softplus2(z2):
    return jnp.maximum(z2, 0.0) + jnp.log2(1.0 + jnp.exp2(-jnp.abs(z2)))


def _sigmoid(z):
    return 1.0 / (1.0 + jnp.exp(-z))


def _neg_tri(n):
    r = lax.broadcasted_iota(jnp.int32, (n, n), 0)
    c = lax.broadcasted_iota(jnp.int32, (n, n), 1)
    return jnp.where(r > c, -1.0, 0.0).astype(BF16)


def _rms(x):
    return x * lax.rsqrt(jnp.mean(x * x, axis=-1, keepdims=True) + EPS)


def _prenorm_kernel(x_ref, g_ref, scale_ref, shift_ref, h_ref):
    h = _rms(x_ref[...]) * g_ref[...]
    h_ref[...] = (h * (1.0 + scale_ref[...]) + shift_ref[...]).astype(h_ref.dtype)


def _finalnorm_kernel(x_ref, g_ref, y_ref):
    y_ref[...] = _rms(x_ref[...]) * g_ref[...]


def _row_tile(s):
    return min(s, 1024)


def _prenorm(x, g, scale, shift):
    b, s, d = x.shape
    tm = _row_tile(s)
    r = scale.shape[1]
    mod_spec = pl.BlockSpec((None, r, d), lambda i, j: (i, 0, 0))
    return pl.pallas_call(
        _prenorm_kernel,
        grid=(b, s // tm),
        in_specs=[pl.BlockSpec((None, tm, d), lambda i, j: (i, j, 0)),
                  pl.BlockSpec((1, d), lambda i, j: (0, 0)),
                  mod_spec, mod_spec],
        out_specs=pl.BlockSpec((None, tm, d), lambda i, j: (i, j, 0)),
        out_shape=jax.ShapeDtypeStruct((b, s, d), BF16),
        compiler_params=_params(2),
        name="prenorm",
    )(x, g, scale, shift)


def _finalnorm(x, g):
    b, s, d = x.shape
    tm = _row_tile(s)
    return pl.pallas_call(
        _finalnorm_kernel,
        grid=(b, s // tm),
        in_specs=[pl.BlockSpec((None, tm, d), lambda i, j: (i, j, 0)),
                  pl.BlockSpec((1, d), lambda i, j: (0, 0))],
        out_specs=pl.BlockSpec((None, tm, d), lambda i, j: (i, j, 0)),
        out_shape=jax.ShapeDtypeStruct((b, s, d), F32),
        compiler_params=_params(2),
        name="finalnorm",
    )(x, g)


def _cast_weight(wb_ref, w_ref):
    def body(i, carry):
        r = pl.multiple_of(i * CAST_ROWS, CAST_ROWS)
        wb_ref[pl.ds(r, CAST_ROWS), :] = w_ref[pl.ds(r, CAST_ROWS), :].astype(BF16)
        return carry
    lax.fori_loop(0, w_ref.shape[0] // CAST_ROWS, body, 0)


def _ws_kernel(*refs, body, side_body, counts):
    refs = iter(refs)
    acts, ws, others, side_acts, side_others, outs, side_outs = (
        [next(refs) for _ in range(n)] for n in counts)
    scratch = list(refs)
    wbs, rest = scratch[:len(ws)], scratch[len(ws):]

    @pl.when(pl.program_id(1) == 0)
    def _():
        for wb, w in zip(wbs, ws):
            _cast_weight(wb, w)
        if side_body is not None:
            side_body(side_acts, wbs, side_others, side_outs, rest)

    body(acts, wbs, others, outs, rest)


def _ws_call(body, *, name, m, tm, tn, n_tiles, tiles_per_batch, acts, weights,
             rows=(), tiles=(), outs=(), extra_outs=(), extra_scratch=(), side=None):
    tpb = tiles_per_batch
    in_specs, args = [], []
    for a in acts:
        in_specs.append(pl.BlockSpec((tm, a.shape[1]), lambda n, i: (i, 0)))
        args.append(a)
    for w, off in weights:
        in_specs.append(pl.BlockSpec((w.shape[0], tn), lambda n, i, off=off: (0, off + n)))
        args.append(w)
    for r in rows:
        in_specs.append(pl.BlockSpec((None, r.shape[1], tn), lambda n, i: (i // tpb, 0, n)))
        args.append(r)
    for t in tiles:
        in_specs.append(pl.BlockSpec((tm, tn), lambda n, i: (i, n)))
        args.append(t)
    n_main_in = len(args)
    out_specs = [pl.BlockSpec((tm, tn), lambda n, i: (i, n)) for _ in outs]
    out_shape = [jax.ShapeDtypeStruct((m, cols), dt) for cols, dt in outs]
    for shape, dt, spec in extra_outs:
        out_specs.append(spec)
        out_shape.append(jax.ShapeDtypeStruct(shape, dt))
    n_main_out = len(out_shape)
    side = side or dict(body=None, acts=(), rows=(), tiles=(), outs=())
    for a in side["acts"]:
        in_specs.append(pl.BlockSpec(a.shape, lambda n, i: (0, 0)))
        args.append(a)
    for r in side["rows"]:
        in_specs.append(pl.BlockSpec((None, r.shape[1], tn), lambda n, i: (0, 0, n)))
        args.append(r)
    for t in side["tiles"]:
        in_specs.append(pl.BlockSpec((t.shape[0], tn), lambda n, i: (0, n)))
        args.append(t)
    for cols, dt in side["outs"]:
        side_rows = side["acts"][0].shape[0]
        out_specs.append(pl.BlockSpec((side_rows, tn), lambda n, i: (0, n)))
        out_shape.append(jax.ShapeDtypeStruct((side_rows, cols), dt))
    counts = (len(acts), len(weights), n_main_in - len(acts) - len(weights),
              len(side["acts"]), len(side["rows"]) + len(side["tiles"]),
              n_main_out, len(side["outs"]))
    scratch = [pltpu.VMEM((w.shape[0], tn), BF16) for w, _ in weights] + list(extra_scratch)
    kern = functools.partial(_ws_kernel, body=body, side_body=side["body"], counts=counts)
    return pl.pallas_call(
        kern,
        grid=(n_tiles, m // tm),
        in_specs=in_specs,
        out_specs=out_specs,
        out_shape=out_shape,
        scratch_shapes=scratch,
        compiler_params=_params(2),
        name=name,
    )(*args)


def _dot(a, wb_ref):
    return jnp.dot(a, wb_ref[...], preferred_element_type=F32)


def _mod_body(acts, wbs, others, outs, scratch):
    c = acts[0][...]
    a = (c * _sigmoid(c)).astype(BF16)
    outs[0][...] = _dot(a, wbs[0]) + others[0][...]


def _qkv_body(acts, wbs, others, outs, scratch):
    a = acts[0][...]
    outs[0][...] = _dot(a, wbs[0]).astype(BF16)
    k = _dot(a, wbs[1])
    v = _dot(a, wbs[2])
    outs[1][...] = k
    outs[2][...] = v
    outs[3][...] = k.astype(BF16)
    outs[4][...] = v.astype(BF16)


def _conv_prompt_body(acts, wbs, others, outs, scratch, *, tpb):
    cw_ref, = others
    o_ref, state_ref = outs
    tail_ref, = scratch
    a = acts[0][...]
    tm = a.shape[0]
    pos = pl.program_id(1) % tpb

    @pl.when(pos == 0)
    def _():
        tail_ref[...] = jnp.zeros(tail_ref.shape, F32)

    u = _dot(a, wbs[1]) * _dot(a, wbs[2])
    prev1 = tail_ref[SUBLANES - 1:SUBLANES, :]
    prev2 = tail_ref[SUBLANES - 2:SUBLANES - 1, :]
    row = lax.broadcasted_iota(jnp.int32, u.shape, 0)
    u1 = jnp.where(row == 0, prev1, pltpu.roll(u, 1, 0))
    u2 = jnp.where(row == 0, prev2, jnp.where(row == 1, prev1, pltpu.roll(u, 2, 0)))
    cw = cw_ref[...]
    conv = cw[0:1, :] * u2 + cw[1:2, :] * u1 + cw[2:3, :] * u
    o_ref[...] = (_dot(a, wbs[0]) * conv).astype(o_ref.dtype)
    tail_ref[...] = u[tm - SUBLANES:, :]

    @pl.when(pos == tpb - 1)
    def _():
        state_ref[...] = u[tm - (CONV_W - 1):, :]


def _conv_sample_body(acts, wbs, others, outs, scratch):
    cw_ref, = others[:1]
    p0_ref, p1_ref = others[1:]
    o_ref, u_ref = outs
    a = acts[0][...]
    u = _dot(a, wbs[1]) * _dot(a, wbs[2])
    cw = cw_ref[...]
    conv = cw[0:1, :] * p0_ref[...] + cw[1:2, :] * p1_ref[...] + cw[2:3, :] * u
    o_ref[...] = (_dot(a, wbs[0]) * conv).astype(o_ref.dtype)
    u_ref[...] = u


def _merge_body(acts, wbs, others, outs, scratch):
    h = acts[0][...]
    o_attn = acts[1][...]
    o_conv = acts[2][...]
    ga = _sigmoid(_dot(h, wbs[0]))
    gb = _sigmoid(_dot(h, wbs[1]))
    merged = ga * _dot(o_attn, wbs[2]) + gb * _dot(o_conv, wbs[3])
    outs[0][...] = merged.astype(outs[0].dtype)


def _residual_body(acts, wbs, others, outs, scratch):
    gate_ref, x_ref = others
    outs[0][...] = x_ref[...] + gate_ref[...] * _dot(acts[0][...], wbs[0])


def _residual_prenorm_body(acts, wbs, others, outs, scratch):
    gate_ref, g_ref, scale_ref, shift_ref, x_ref = others
    x1 = x_ref[...] + gate_ref[...] * _dot(acts[0][...], wbs[0])
    outs[0][...] = x1
    h = _rms(x1) * g_ref[...]
    outs[1][...] = (h * (1.0 + scale_ref[...]) + shift_ref[...]).astype(outs[1].dtype)


def _swiglu_body(acts, wbs, others, outs, scratch):
    a = acts[0][...]
    up = _dot(a, wbs[0])
    outs[0][...] = (up * _sigmoid(up) * _dot(a, wbs[1])).astype(outs[0].dtype)


ATTN_BLOCK = 256
ATTN_HEADS = 4
PAGE_UNIT = 2
PAGE_SLOTS = 8


def _attn_kernel(pt_ref, bias_ref, q_ref, k_ref, v_ref, sbias_ref, sq_ref, ck_ref, cv_ref,
                 o_ref, so_ref,
                 tri_ref, acc_ref, run_ref,
                 kbuf, vbuf, sem, stri_ref, sacc_ref, srun_ref, *, tq, hps):
    lin = pl.program_id(0) * pl.num_programs(1) + pl.program_id(1)
    group = pl.program_id(1)
    qi = pl.program_id(2)
    n_q = pl.num_programs(2)
    row = lax.broadcasted_iota(jnp.int32, (tq, tq), 0)
    col = lax.broadcasted_iota(jnp.int32, (tq, tq), 1)
    causal = col < row
    scale2 = HEAD_DIM ** -0.5 * LOG2E

    n_seq, n_pages = pt_ref.shape
    unit, n_heads, page, _ = kbuf.shape[1:]
    units_per_seq = n_pages // unit
    n_units = n_seq * units_per_seq
    unit_rows = unit * n_heads * page
    head_row = lax.broadcasted_iota(jnp.int32, (n_heads, page), 0)

    def unit_copies(u):
        first_page = n_pages - (u % units_per_seq + 1) * unit
        slot = u % PAGE_SLOTS
        copies = []
        for i in range(unit):
            page_id = pt_ref[u // units_per_seq, first_page + i]
            for h in range(n_heads):
                copies.append(pltpu.make_async_copy(
                    ck_ref.at[page_id, :, h, :], kbuf.at[slot, i, h], sem.at[slot]))
                copies.append(pltpu.make_async_copy(
                    cv_ref.at[page_id, :, h, :], vbuf.at[slot, i, h], sem.at[slot]))
        return copies

    def start(u):
        for c in unit_copies(u):
            c.start()

    def wait(u):
        for c in unit_copies(u):
            c.wait()

    @pl.when((lin == 0) & (qi == 0))
    def _():
        for u in range(PAGE_SLOTS - 1):
            start(u)
        stri_ref[...] = _neg_tri(page)
        sacc_ref[...] = jnp.zeros(sacc_ref.shape, F32)
        srun_ref[...] = jnp.zeros(srun_ref.shape, F32)

    @pl.when(qi == 0)
    def _():
        tri_ref[...] = _neg_tri(tq)

    def maybe(cond, guarded, fn):
        if guarded:
            pl.when(cond)(fn)
        else:
            fn()

    heads = range(hps)
    lanes = [slice(i * HEAD_DIM, (i + 1) * HEAD_DIM) for i in heads]
    bias2 = [bias_ref[0, group * hps + i] * LOG2E for i in heads]
    sbias2 = jnp.concatenate([sbias_ref[...]] * unit, axis=0) * LOG2E

    def iteration(block, t, diagonal, guarded):
        start_row = pl.multiple_of(block * tq, tq)
        live = t < n_units
        slot = t % PAGE_SLOTS
        seq = jnp.minimum(t // units_per_seq, n_seq - 1)
        maybe(live, guarded, lambda: wait(t))

        z2 = [lax.dot_general(q_ref[:, lanes[i]], k_ref[pl.ds(start_row, tq), lanes[i]],
                              (((1,), (1,)), ((), ())), preferred_element_type=F32)
              * scale2 + bias2[i] for i in heads]
        kall = kbuf[slot].reshape(unit_rows, HEAD_DIM).astype(BF16)
        cross = lax.dot_general(sq_ref[seq].astype(BF16), kall, (((1,), (1,)), ((), ())),
                                preferred_element_type=F32)

        log_beta, drop, split = [], [], []
        for i in heads:
            sp2 = _softplus2(z2[i])
            log_beta.append(z2[i] - sp2)
            drop.append(jnp.where(causal, sp2, 0.0) if diagonal else sp2)
            split.append(drop[i].astype(BF16))
        pieces = []
        for i in range(unit):
            s = jnp.zeros((n_heads, page), F32)
            for h in range(n_heads):
                c0 = (i * n_heads + h) * page
                s = s + jnp.where(head_row == h, cross[:, c0:c0 + page], 0.0)
            pieces.append(s)
        sz2 = jnp.concatenate(pieces, axis=0) * scale2 + sbias2
        ssp2 = _softplus2(sz2)
        ssplit = ssp2.astype(BF16)

        suffix = [jnp.dot(s, tri_ref[...], preferred_element_type=F32) for s in split]
        ssuffix = jnp.dot(ssplit, stri_ref[...], preferred_element_type=F32)

        if diagonal:
            w = [jnp.where(causal, jnp.exp2(log_beta[i] + suffix[i]), 0.0) for i in heads]
        else:
            w = [jnp.exp2(log_beta[i] + suffix[i] + run_ref[i]) for i in heads]
        page_sum = jnp.sum(ssp2, axis=1, keepdims=True)
        fresh = live & (t % units_per_seq == 0)
        tail = jnp.where(fresh, 0.0, srun_ref[...])
        later = [None] * unit
        for i in reversed(range(unit)):
            later[i] = tail
            tail = tail - page_sum[i * n_heads:(i + 1) * n_heads]
        srun_ref[...] = tail
        sw = jnp.exp2((sz2 - ssp2) + ssuffix + jnp.concatenate(later, axis=0))
        sw = jnp.where(live, sw, 0.0)
        sw_blocks = jnp.concatenate(
            [jnp.where(head_row == h, sw[i * n_heads:(i + 1) * n_heads], 0.0)
             for i in range(unit) for h in range(n_heads)], axis=1).astype(BF16)

        pv = [jnp.dot(w[i].astype(BF16), v_ref[pl.ds(start_row, tq), lanes[i]],
                      preferred_element_type=F32) for i in heads]
        vall = vbuf[slot].reshape(unit_rows, HEAD_DIM).astype(BF16)
        sacc = (jnp.where(fresh, 0.0, sacc_ref[...])
                + jnp.dot(sw_blocks, vall, preferred_element_type=F32))
        sacc_ref[...] = sacc
        so_ref[seq] = sacc
        for i in heads:
            mass = jnp.sum(drop[i], axis=1, keepdims=True)
            if diagonal:
                acc_ref[i] = pv[i]
                run_ref[i] = -mass
            else:
                acc_ref[i] += pv[i]
                run_ref[i] -= mass
        ahead = t + PAGE_SLOTS - 1
        maybe(ahead < n_units, guarded, lambda: start(ahead))

    t0 = lin * (n_q * (n_q + 1) // 2) + qi * (qi + 1) // 2

    def step(guarded):
        iteration(qi, t0, True, guarded)

        def body(j, carry):
            iteration(qi - 1 - j, t0 + 1 + j, False, guarded)
            return carry

        lax.fori_loop(0, qi, body, 0)
        for i in heads:
            o_ref[:, lanes[i]] = acc_ref[i].astype(o_ref.dtype)

    unguarded = t0 + qi + PAGE_SLOTS - 1 < n_units
    pl.when(unguarded)(lambda: step(False))
    pl.when(jnp.logical_not(unguarded))(lambda: step(True))


def _attention(q, k, v, bias, sq, cache_k, cache_v, page_table, sbias_rows):
    b, s, d = q.shape
    tq, hps = ATTN_BLOCK, ATTN_HEADS
    n_heads = d // HEAD_DIM
    width = hps * HEAD_DIM
    db = sq.shape[0]
    page = cache_k.shape[1]
    n_pages = page_table.shape[1]
    n_q = s // tq
    iterations = b * (n_heads // hps) * (n_q * (n_q + 1) // 2)
    assert n_pages % PAGE_UNIT == 0 and s % tq == 0 and n_heads % hps == 0
    n_units = db * (n_pages // PAGE_UNIT)
    assert PAGE_SLOTS - 1 <= n_units <= iterations
    kv_spec = pl.BlockSpec((None, s, width), lambda i, g, j, pt: (i, 0, g))
    q_spec = pl.BlockSpec((None, tq, width), lambda i, g, j, pt: (i, j, g))
    whole = lambda shape: pl.BlockSpec(shape, lambda i, g, j, pt: (0,) * len(shape))
    slot_shape = (PAGE_SLOTS, PAGE_UNIT, n_heads, page, HEAD_DIM)
    grid_spec = pltpu.PrefetchScalarGridSpec(
        num_scalar_prefetch=1,
        grid=(b, n_heads // hps, n_q),
        in_specs=[pl.BlockSpec(memory_space=pltpu.SMEM), q_spec, kv_spec, kv_spec,
                  whole((n_heads, page)), whole((db, n_heads, HEAD_DIM)),
                  pl.BlockSpec(memory_space=pl.ANY), pl.BlockSpec(memory_space=pl.ANY)],
        out_specs=[q_spec, whole((db, n_heads, HEAD_DIM))],
        scratch_shapes=[pltpu.VMEM((tq, tq), BF16),
                        pltpu.VMEM((hps, tq, HEAD_DIM), F32),
                        pltpu.VMEM((hps, tq, 1), F32),
                        pltpu.VMEM(slot_shape, F32),
                        pltpu.VMEM(slot_shape, F32),
                        pltpu.SemaphoreType.DMA((PAGE_SLOTS,)),
                        pltpu.VMEM((page, page), BF16),
                        pltpu.VMEM((n_heads, HEAD_DIM), F32),
                        pltpu.VMEM((n_heads, 1), F32)])
    return pl.pallas_call(
        functools.partial(_attn_kernel, tq=tq, hps=hps),
        grid_spec=grid_spec,
        out_shape=[jax.ShapeDtypeStruct((b, s, d), BF16),
                   jax.ShapeDtypeStruct((db, n_heads, HEAD_DIM), F32)],
        compiler_params=_params(3),
        name="attention",
    )(page_table, bias, q, k, v, sbias_rows, sq, cache_k, cache_v)


def _col_tiles(offset, tn):
    assert offset % tn == 0
    return offset // tn


TM_WIDE = 1024
TM_NARROW = 512
TN = 512
TM_FULL_ROWS = 512


def _layer(xp, xs, mods_p, mods_s, lw, cw, bias_smem, bias_rows, ck, cv, page_table, p0, p1, db):
    (g_mix, g_ffn, w_in, w_attn_out, w_conv_out, w_out, w_ffn_in, w_ffn_out) = lw
    shift1_p, scale1_p, gate1_p, shift2_p, scale2_p, gate2_p = mods_p
    shift1_s, scale1_s, gate1_s, shift2_s, scale2_s, gate2_s = mods_s
    bp, sp, d = xp.shape
    m = bp * sp
    n_heads = bias_smem.shape[1]
    d_attn = w_attn_out.shape[0]
    d_conv = w_conv_out.shape[0]
    d_ff = w_ffn_out.shape[0]
    off_k, off_v = d_attn, 2 * d_attn
    off_cb = 3 * d_attn
    off_cc, off_ch = off_cb + d_conv, off_cb + 2 * d_conv
    off_ga = off_cb + 3 * d_conv
    off_gb = off_ga + d

    def fused(body, name, *, tm, tn, cols, acts, side_acts, weights, outs, rows=(), side_rows=(),
              tiles=(), side_tiles=(), side_body=None, side_outs=None, extra_outs=(),
              extra_scratch=()):
        res = _ws_call(
            body, name=name, m=m, tm=tm, tn=tn, n_tiles=cols // tn, tiles_per_batch=sp // tm,
            acts=acts, weights=weights, rows=rows, tiles=tiles, outs=outs,
            extra_outs=extra_outs, extra_scratch=extra_scratch,
            side=dict(body=side_body or body, acts=side_acts, rows=side_rows, tiles=side_tiles,
                      outs=outs if side_outs is None else side_outs))
        n_main = len(outs) + len(extra_outs)
        return res[:n_main], res[n_main:]

    def pad_rows(a):
        return jnp.pad(a, ((0, SAMPLE_ROWS - a.shape[0]), (0, 0)))

    h_p = _prenorm(xp, g_mix, scale1_p, shift1_p).reshape(m, d)
    h_s = _prenorm(xs, g_mix, scale1_s, shift1_s).reshape(SAMPLE_ROWS, d)
    (q_p, k_p, v_p, kb_p, vb_p), (q_s, k_s, v_s, _, _) = fused(
        _qkv_body, "qkv", tm=TM_NARROW, tn=TN, cols=d_attn, acts=[h_p], side_acts=[h_s],
        weights=[(w_in, 0), (w_in, _col_tiles(off_k, TN)), (w_in, _col_tiles(off_v, TN))],
        outs=[(d_attn, BF16), (d_attn, F32), (d_attn, F32), (d_attn, BF16), (d_attn, BF16)])

    shp = (bp, sp, d_attn)
    q3 = q_s[:db].astype(F32).reshape(db, n_heads, HEAD_DIM)
    o_attn_p, o_attn_s = _attention(q_p.reshape(shp), kb_p.reshape(shp), vb_p.reshape(shp),
                                    bias_smem, q3, ck, cv, page_table, bias_rows)
    o_attn_p = o_attn_p.reshape(m, d_attn)
    o_attn_s = pad_rows(o_attn_s.reshape(db, d_attn)).astype(BF16)

    tpb_conv = sp // TM_WIDE
    state_spec = pl.BlockSpec((None, CONV_W - 1, TN), lambda n, i: (i // tpb_conv, 0, n))
    cw_rows = cw.reshape(1, SUBLANES, d_conv)
    (o_conv_p, state_p), (o_conv_s, u_s) = fused(
        functools.partial(_conv_prompt_body, tpb=tpb_conv), "conv", tm=TM_WIDE, tn=TN,
        cols=d_conv, acts=[h_p], side_acts=[h_s],
        weights=[(w_in, _col_tiles(off_cb, TN)), (w_in, _col_tiles(off_cc, TN)),
                 (w_in, _col_tiles(off_ch, TN))],
        rows=[jnp.broadcast_to(cw_rows, (bp, SUBLANES, d_conv))], side_rows=[cw_rows],
        side_tiles=[p0, p1], outs=[(d_conv, BF16)],
        extra_outs=[((bp, CONV_W - 1, d_conv), F32, state_spec)],
        extra_scratch=[pltpu.VMEM((SUBLANES, TN), F32)],
        side_body=_conv_sample_body, side_outs=[(d_conv, BF16), (d_conv, F32)])
    state_s = jnp.stack([p1[:db], u_s[:db]], axis=1)

    (merged_p,), (merged_s,) = fused(
        _merge_body, "merge", tm=TM_NARROW, tn=TN, cols=d,
        acts=[h_p, o_attn_p, o_conv_p], side_acts=[h_s, o_attn_s, o_conv_s],
        weights=[(w_in, _col_tiles(off_ga, TN)), (w_in, _col_tiles(off_gb, TN)),
                 (w_attn_out, 0), (w_conv_out, 0)],
        outs=[(d, BF16)])
    g_rows = g_ffn.reshape(1, 1, d)
    (x1_p, h2_p), (x1_s, h2_s) = fused(
        _residual_prenorm_body, "mix_out", tm=TM_FULL_ROWS, tn=d, cols=d, acts=[merged_p],
        side_acts=[merged_s], weights=[(w_out, 0)],
        rows=[gate1_p, jnp.broadcast_to(g_rows, (bp, 1, d)), scale2_p, shift2_p],
        side_rows=[gate1_s, g_rows, scale2_s, shift2_s],
        tiles=[xp.reshape(m, d)], side_tiles=[xs.reshape(SAMPLE_ROWS, d)],
        outs=[(d, F32), (d, BF16)])
    (act_p,), (act_s,) = fused(
        _swiglu_body, "ffn_in", tm=TM_WIDE, tn=TN, cols=d_ff, acts=[h2_p],
        side_acts=[h2_s],
        weights=[(w_ffn_in, 0), (w_ffn_in, _col_tiles(d_ff, TN))], outs=[(d_ff, BF16)])
    (x2_p,), (x2_s,) = fused(
        _residual_body, "ffn_out", tm=TM_NARROW, tn=TN, cols=d, acts=[act_p], side_acts=[act_s],
        weights=[(w_ffn_out, 0)], rows=[gate2_p], side_rows=[gate2_s],
        tiles=[x1_p], side_tiles=[x1_s], outs=[(d, F32)])
    return (x2_p.reshape(bp, sp, d), x2_s.reshape(1, SAMPLE_ROWS, d),
            k_p, v_p, state_p, k_s, v_s, state_s)


def kernel(x_prompt, x_sample, cache_k, cache_v, state_conv, page_table, c_prompt, c_sample,
           g_mix, g_ffn, g_final, w_mod, b_mod, w_in, b_sb, conv_w, w_attn_out, w_conv_out,
           w_out, w_ffn_in, w_ffn_out):
    depth = w_in.shape[0]
    bp, sp, d = x_prompt.shape
    db, ds, _ = x_sample.shape
    assert ds == 1 and db <= SAMPLE_ROWS
    n_heads = b_sb.shape[1]
    page = cache_k.shape[2]
    pad = SAMPLE_ROWS - db

    def pad_rows(a):
        return jnp.pad(a, ((0, pad), (0, 0)))

    xp = x_prompt
    xs = pad_rows(x_sample.reshape(db, d)).reshape(1, SAMPLE_ROWS, d)
    c_all = jnp.concatenate([c_prompt, c_sample], axis=0)
    c_all = jnp.pad(c_all, ((0, -c_all.shape[0] % SAMPLE_ROWS), (0, 0)))
    g_final2 = g_final.reshape(1, d)
    outs = {k: [] for k in ("kp", "vp", "cp", "ks", "vs", "cs")}

    for l in range(depth):
        mod, = _ws_call(
            _mod_body, name="mod", m=c_all.shape[0], tm=c_all.shape[0], tn=1024,
            n_tiles=N_MOD * d // 1024, tiles_per_batch=1, acts=[c_all],
            weights=[(w_mod[l], 0)], rows=[b_mod[l].reshape(1, 1, N_MOD * d)],
            outs=[(N_MOD * d, F32)])
        mods_p = [mod[:bp, i * d:(i + 1) * d].reshape(bp, 1, d) for i in range(N_MOD)]
        mods_s = [pad_rows(mod[bp:bp + db, i * d:(i + 1) * d]).reshape(1, SAMPLE_ROWS, d)
                  for i in range(N_MOD)]
        lw = (g_mix[l].reshape(1, d), g_ffn[l].reshape(1, d), w_in[l], w_attn_out[l],
              w_conv_out[l], w_out[l], w_ffn_in[l], w_ffn_out[l])
        cw = jnp.pad(conv_w[l], ((0, SUBLANES - CONV_W), (0, 0)))
        bias_smem = b_sb[l].reshape(1, n_heads)
        bias_rows = jnp.broadcast_to(b_sb[l].reshape(n_heads, 1), (n_heads, page))
        p0 = pad_rows(state_conv[l][:, 0, :])
        p1 = pad_rows(state_conv[l][:, 1, :])
        xp, xs, kp, vp, csp, ks, vs, css = _layer(
            xp, xs, mods_p, mods_s, lw, cw, bias_smem, bias_rows, cache_k[l], cache_v[l],
            page_table, p0, p1, db)

        outs["kp"].append(kp.reshape(bp, sp, n_heads, HEAD_DIM))
        outs["vp"].append(vp.reshape(bp, sp, n_heads, HEAD_DIM))
        outs["cp"].append(csp)
        outs["ks"].append(ks[:db].reshape(db, 1, n_heads, HEAD_DIM))
        outs["vs"].append(vs[:db].reshape(db, 1, n_heads, HEAD_DIM))
        outs["cs"].append(css)

    y_prompt = _finalnorm(xp, g_final2)
    y_sample = _finalnorm(xs, g_final2)[0, :db].reshape(db, 1, d)
    return (y_prompt, y_sample, jnp.stack(outs["kp"]), jnp.stack(outs["vp"]),
            jnp.stack(outs["cp"]), jnp.stack(outs["ks"]), jnp.stack(outs["vs"]),
            jnp.stack(outs["cs"]))
```
